```python
import jax, jax.numpy as jnp
from jax import lax
import numpy as np

D_MODEL = 2048
BATCH = 16
SEQ = 256
DEPTH = 2
DEC_BATCH = 8
DEC_SEQ = 4096
PAST_LEN = 256

GRID_W = 64
EPS = 1e-6
N_HEADS = 16
QK_NOPE = 64
QK_ROPE = 32
QK_HEAD = QK_NOPE + QK_ROPE
V_HEAD = 64
Q_LORA = 768
KV_LORA = 256
ROPE_THETA = 10000.0
AXIS_FREQS = QK_ROPE // 4
Q_BLOCK = 128
ATTN_SCALE = QK_HEAD ** -0.5
ATTN_W = N_HEADS * V_HEAD
POOL_WINDOWS = (2, 4, 8, 16)
POOL_GROUP_W = 128
POOL_W = POOL_GROUP_W * len(POOL_WINDOWS)
CHUNK = 128
SGU_GROUPS = 4
SGU_W = 512
SGU_GROUP_W = SGU_W // SGU_GROUPS
CONV_W = 512
CONV_K = 3
N_BRANCH = 4
D_FF = -(-8 * D_MODEL // (3 * 256)) * 256
OFF_CKV = Q_LORA
OFF_KPE = OFF_CKV + KV_LORA
OFF_POOL = OFF_KPE + QK_ROPE
OFF_SGU = OFF_POOL + POOL_W
OFF_CONV = OFF_SGU + 2 * SGU_W
OFF_GATE = OFF_CONV + 3 * CONV_W
N_IN = OFF_GATE + N_BRANCH * D_MODEL

kernel_name = "hybrid_diffusion_prefix_mla_pool_sgu_conv_step"


def rmsnorm(x, w):
    xf = x.astype(jnp.float32)
    y = xf * lax.rsqrt(jnp.mean(xf * xf, axis=-1, keepdims=True) + EPS)
    return (y * w.astype(jnp.float32)).astype(x.dtype)


def ada_params(cond, w_mod, b_mod):
    m = jax.nn.silu(cond) @ w_mod + b_mod
    m = m.reshape(m.shape[:-1] + (6, D_MODEL))
    return [m[..., i, :] for i in range(6)]


def modulate(x, norm_w, shift, scale):
    return rmsnorm(x, norm_w) * (1 + scale) + shift


def axial_rope_tables(T):
    rows = T // GRID_W
    row = jnp.repeat(jnp.arange(rows), GRID_W).astype(jnp.float32)
    col = jnp.tile(jnp.arange(GRID_W), rows).astype(jnp.float32)
    inv = ROPE_THETA ** (-jnp.arange(AXIS_FREQS, dtype=jnp.float32) / AXIS_FREQS)
    ang_r = row[:, None] * inv
    ang_c = col[:, None] * inv
    ang = jnp.stack([ang_r, ang_r, ang_c, ang_c], axis=1).reshape(T, QK_ROPE)
    return jnp.cos(ang), jnp.sin(ang)


def rope_rotary_part(x, cos, sin):
    xr = x[..., QK_NOPE:].astype(jnp.float32)
    xa = xr.reshape(xr.shape[:-1] + (2, 2, AXIS_FREQS))
    rot = jnp.stack([-xa[..., 1, :], xa[..., 0, :]], axis=-2).reshape(xr.shape)
    xr = xr * cos[None, :, None, :] + rot * sin[None, :, None, :]
    return jnp.concatenate([x[..., :QK_NOPE], xr.astype(x.dtype)], axis=-1)


def mla_latents(z, lw):
    c_q = rmsnorm(z[..., :OFF_CKV], lw["q_a_norm_w"])
    c_kv = rmsnorm(z[..., OFF_CKV:OFF_KPE], lw["kv_a_norm_w"])
    k_pe = z[..., OFF_KPE:OFF_POOL]
    return c_q, c_kv, k_pe


def mla_queries(c_q, w_uq, q_norm_w):
    B, T, _ = c_q.shape
    q = (c_q @ w_uq).reshape(B, T, N_HEADS, QK_HEAD)
    return rmsnorm(q, q_norm_w)


def mla_keys_values(c_kv, k_pe, w_ukv, k_norm_w):
    B, T, _ = c_kv.shape
    kv = (c_kv @ w_ukv).reshape(B, T, N_HEADS, QK_NOPE + V_HEAD)
    k_nope, v = kv[..., :QK_NOPE], kv[..., QK_NOPE:]
    k_rope = jnp.broadcast_to(k_pe[:, :, None, :], (B, T, N_HEADS, QK_ROPE))
    k = rmsnorm(jnp.concatenate([k_nope, k_rope], axis=-1), k_norm_w)
    return k, v


def attend_block(qb, k, v):
    s = jnp.einsum("bqhd,bkhd->bhqk", qb, k).astype(jnp.float32) * ATTN_SCALE
    p = jax.nn.softmax(s, axis=-1)
    return jnp.einsum("bhqk,bkhv->bqhv", p.astype(v.dtype), v)


def attend_blocked(q, k, v):
    B, T, H, Dh = q.shape
    nb = T // Q_BLOCK
    qb = q.reshape(B, nb, Q_BLOCK, H, Dh).transpose(1, 0, 2, 3, 4)
    ob = lax.map(lambda qi: attend_block(qi, k, v), qb)
    return ob.transpose(1, 0, 2, 3, 4).reshape(B, T, H * V_HEAD)


def multi_scale_pool(xp, w_pool, pool_scale):
    B, T, C = xp.shape
    xf = xp.astype(jnp.float32)
    S = jnp.concatenate([jnp.zeros((B, 1, C), jnp.float32), jnp.cumsum(xf, axis=1)], axis=1)
    t = jnp.arange(T)
    outs = []
    for g, w in enumerate(POOL_WINDOWS):
        lo = jnp.clip(t - w // 2, 0, T - 1)
        hi = jnp.clip(t + w // 2 - 1, 0, T - 1)
        sl = slice(g * POOL_GROUP_W, (g + 1) * POOL_GROUP_W)
        Sg = S[..., sl]
        cnt = (hi - lo + 1).astype(jnp.float32)[None, :, None]
        pooled = (Sg[:, hi + 1] - Sg[:, lo]) / cnt - xf[..., sl]
        outs.append(pooled.astype(xp.dtype) @ w_pool[g])
    return jnp.concatenate(outs, axis=-1) * pool_scale


def spatial_gating(zs, sgu_norm_w, w_spatial, b_spatial):
    zs = jax.nn.gelu(zs)
    u, v = zs[..., :SGU_W], zs[..., SGU_W:]
    v = rmsnorm(v, sgu_norm_w)
    B, T, _ = v.shape
    vc = v.reshape(B, T // CHUNK, CHUNK, SGU_GROUPS, SGU_GROUP_W)
    mixed = jnp.einsum("gqp,bnpgc->bnqgc", w_spatial, vc) + b_spatial.T[:, :, None]
    return u * mixed.reshape(B, T, SGU_W)


def short_conv(zc, conv_w):
    b_g = zc[..., :CONV_W]
    c_g = zc[..., CONV_W:2 * CONV_W]
    xc = zc[..., 2 * CONV_W:]
    y = c_g * xc
    yp = jnp.pad(y, ((0, 0), (1, 1), (0, 0)))
    conv = conv_w[0] * yp[:, :-2] + conv_w[1] * yp[:, 1:-1] + conv_w[2] * yp[:, 2:]
    return b_g * conv


def local_branches(z, lw):
    pool_o = multi_scale_pool(z[..., OFF_POOL:OFF_SGU], lw["w_pool"], lw["pool_scale"])
    sgu_o = spatial_gating(z[..., OFF_SGU:OFF_CONV], lw["sgu_norm_w"], lw["w_spatial"], lw["b_spatial"])
    conv_o = short_conv(z[..., OFF_CONV:OFF_GATE], lw["conv_w"])
    return pool_o, sgu_o, conv_o


def merge_branches(z, attn_o, pool_o, sgu_o, conv_o, lw):
    B, T, _ = z.shape
    g = jax.nn.sigmoid(z[..., OFF_GATE:] + lw["b_gate"]).reshape(B, T, N_BRANCH, D_MODEL)
    m = (g[..., 0, :] * (attn_o @ lw["w_br_attn"]) + g[..., 1, :] * (pool_o @ lw["w_br_pool"])
         + g[..., 2, :] * (sgu_o @ lw["w_br_sgu"]) + g[..., 3, :] * (conv_o @ lw["w_br_conv"]))
    return m @ lw["w_out"]


def swiglu(h, w_ffn_in, w_ffn_out):
    a = h @ w_ffn_in
    return (jax.nn.silu(a[..., :D_FF]) * a[..., D_FF:]) @ w_ffn_out


def context_layer(x, c_ctx, lw):
    sh_a, sc_a, g_a, sh_f, sc_f, g_f = ada_params(c_ctx, lw["w_mod"], lw["b_mod"])
    h = modulate(x, lw["norm_mix_w"], sh_a, sc_a)
    z = h @ lw["w_in"]
    c_q, c_kv, k_pe = mla_latents(z, lw)
    q = mla_queries(c_q, lw["w_uq"], lw["q_norm_w"])
    k, v = mla_keys_values(c_kv, k_pe, lw["w_ukv"], lw["k_norm_w"])
    attn_o = attend_blocked(q, k, v)
    mix = merge_branches(z, attn_o, *local_branches(z, lw), lw)
    x = x + g_a * mix
    x = x + g_f * swiglu(modulate(x, lw["norm_ffn_w"], sh_f, sc_f), lw["w_ffn_in"], lw["w_ffn_out"])
    return x, c_kv, k_pe


def latent_layer(x, cond, ckv_ctx, kpe_ctx, lw, cos, sin):
    sh_a, sc_a, g_a, sh_f, sc_f, g_f = ada_params(cond[:, None, :], lw["w_mod"], lw["b_mod"])
    h = modulate(x, lw["norm_mix_w"], sh_a, sc_a)
    z = h @ lw["w_in"]
    c_q, c_kv, k_pe = mla_latents(z, lw)
    q = rope_rotary_part(mla_queries(c_q, lw["w_uq"], lw["q_norm_w"]), cos, sin)
    k, v = mla_keys_values(c_kv, k_pe, lw["w_ukv"], lw["k_norm_w"])
    k = rope_rotary_part(k, cos, sin)
    k_c, v_c = mla_keys_values(ckv_ctx, kpe_ctx, lw["w_ukv"], lw["k_norm_w"])
    k_all = jnp.concatenate([k, k_c.astype(k.dtype)], axis=1)
    v_all = jnp.concatenate([v, v_c.astype(v.dtype)], axis=1)
    attn_o = attend_blocked(q, k_all, v_all)
    mix = merge_branches(z, attn_o, *local_branches(z, lw), lw)
    x = x + g_a * mix
    x = x + g_f * swiglu(modulate(x, lw["norm_ffn_w"], sh_f, sc_f), lw["w_ffn_in"], lw["w_ffn_out"])
    return x


def setup_inputs(seed: int = 0) -> dict:
    key = jax.random.key(seed)
    ks = iter(jax.random.split(key, 40))
    f32 = jnp.float32

    def nrm(shape, scale):
        return jax.random.normal(next(ks), shape, f32) * scale

    def gain(shape):
        return 1.0 + 0.02 * jax.random.normal(next(ks), shape, f32)

    L = DEPTH
    return {
        "x_prompt": nrm((BATCH, SEQ, D_MODEL), 1.0),
        "x_sample": nrm((DEC_BATCH, DEC_SEQ, D_MODEL), 1.0),
        "cache_ckv": nrm((DEC_BATCH, DEPTH, PAST_LEN, KV_LORA), 1.0),
        "cache_kpe": nrm((DEC_BATCH, DEPTH, PAST_LEN, QK_ROPE), 1.0),
        "c": nrm((DEC_BATCH, D_MODEL), 1.0),
        "c_ctx": nrm((D_MODEL,), 1.0),
        "w_mod": nrm((L, D_MODEL, 6 * D_MODEL), 0.5 * D_MODEL ** -0.5),
        "b_mod": nrm((L, 6 * D_MODEL), 0.02),
        "norm_mix_w": gain((L, D_MODEL)),
        "norm_ffn_w": gain((L, D_MODEL)),
        "w_in": nrm((L, D_MODEL, N_IN), D_MODEL ** -0.5),
        "b_gate": nrm((L, N_BRANCH * D_MODEL), 0.02),
        "q_a_norm_w": gain((L, Q_LORA)),
        "kv_a_norm_w": gain((L, KV_LORA)),
        "w_uq": nrm((L, Q_LORA, N_HEADS * QK_HEAD), Q_LORA ** -0.5),
        "w_ukv": nrm((L, KV_LORA, N_HEADS * (QK_NOPE + V_HEAD)), KV_LORA ** -0.5),
        "q_norm_w": gain((L, QK_HEAD)),
        "k_norm_w": gain((L, QK_HEAD)),
        "w_pool": nrm((L, len(POOL_WINDOWS), POOL_GROUP_W, POOL_GROUP_W), POOL_GROUP_W ** -0.5),
        "pool_scale": gain((L, POOL_W)),
        "sgu_norm_w": gain((L, SGU_W)),
        "w_spatial": nrm((L, SGU_GROUPS, CHUNK, CHUNK), CHUNK ** -0.5),
        "b_spatial": gain((L, SGU_GROUPS, CHUNK)),
        "conv_w": nrm((L, CONV_K, CONV_W), CONV_K ** -0.5),
        "w_br_attn": nrm((L, ATTN_W, D_MODEL), ATTN_W ** -0.5),
        "w_br_pool": nrm((L, POOL_W, D_MODEL), POOL_W ** -0.5),
        "w_br_sgu": nrm((L, SGU_W, D_MODEL), SGU_W ** -0.5),
        "w_br_conv": nrm((L, CONV_W, D_MODEL), CONV_W ** -0.5),
        "w_out": nrm((L, D_MODEL, D_MODEL), D_MODEL ** -0.5),
        "w_ffn_in": nrm((L, D_MODEL, 2 * D_FF), D_MODEL ** -0.5),
        "w_ffn_out": nrm((L, D_FF, D_MODEL), D_FF ** -0.5),
    }


def reference(x_prompt, x_sample, cache_ckv, cache_kpe, c, c_ctx, w_mod, b_mod, norm_mix_w, norm_ffn_w,
              w_in, b_gate, q_a_norm_w, kv_a_norm_w, w_uq, w_ukv, q_norm_w, k_norm_w, w_pool, pool_scale,
              sgu_norm_w, w_spatial, b_spatial, conv_w, w_br_attn, w_br_pool, w_br_sgu, w_br_conv, w_out,
              w_ffn_in, w_ffn_out):
    cos, sin = axial_rope_tables(x_sample.shape[1])
    xp = x_prompt
    xs = x_sample
    ckv_list = []
    kpe_list = []
    for l in range(DEPTH):
        lw = dict(w_mod=w_mod[l], b_mod=b_mod[l], norm_mix_w=norm_mix_w[l], norm_ffn_w=norm_ffn_w[l],
                  w_in=w_in[l], b_gate=b_gate[l], q_a_norm_w=q_a_norm_w[l], kv_a_norm_w=kv_a_norm_w[l],
                  w_uq=w_uq[l], w_ukv=w_ukv[l], q_norm_w=q_norm_w[l], k_norm_w=k_norm_w[l],
                  w_pool=w_pool[l], pool_scale=pool_scale[l], sgu_norm_w=sgu_norm_w[l],
                  w_spatial=w_spatial[l], b_spatial=b_spatial[l], conv_w=conv_w[l],
                  w_br_attn=w_br_attn[l], w_br_pool=w_br_pool[l], w_br_sgu=w_br_sgu[l],
                  w_br_conv=w_br_conv[l], w_out=w_out[l], w_ffn_in=w_ffn_in[l], w_ffn_out=w_ffn_out[l])
        xp, ckv_l, kpe_l = context_layer(xp, c_ctx, lw)
        ckv_list.append(ckv_l)
        kpe_list.append(kpe_l)
        xs = latent_layer(xs, c, cache_ckv[:, l], cache_kpe[:, l], lw, cos, sin)
    state_ckv = jnp.stack(ckv_list, axis=1)
    state_kpe = jnp.stack(kpe_list, axis=1)
    return (xp, xs, state_ckv, state_kpe)
```

```python
import functools

import numpy as np
import jax
import jax.numpy as jnp
from jax import lax
from jax.experimental import pallas as pl
from jax.experimental.pallas import tpu as pltpu

F32 = jnp.float32
BF16 = jnp.bfloat16

D_MODEL = 2048
EPS = 1e-6
GRID_W = 64
N_HEADS = 16
QK_NOPE = 64
QK_ROPE = 32
QK_HEAD = QK_NOPE + QK_ROPE
V_HEAD = 64
Q_LORA = 768
KV_LORA = 256
ROPE_THETA = 10000.0
AXIS_FREQS = QK_ROPE // 4
ATTN_SCALE = QK_HEAD ** -0.5
ATTN_W = N_HEADS * V_HEAD
POOL_WINDOWS = (2, 4, 8, 16)
POOL_GROUP_W = 128
POOL_W = POOL_GROUP_W * len(POOL_WINDOWS)
CHUNK = 128
SGU_GROUPS = 4
SGU_W = 512
CONV_W = 512
N_BRANCH = 4
D_FF = 5632
OFF_CKV = Q_LORA
OFF_KPE = OFF_CKV + KV_LORA
OFF_POOL = OFF_KPE + QK_ROPE
OFF_SGU = OFF_POOL + POOL_W
OFF_CONV = OFF_SGU + 2 * SGU_W
OFF_GATE = OFF_CONV + 3 * CONV_W

LANES = 128
HEAD_PAD = LANES
QK_PAD = N_HEADS * HEAD_PAD
MLA_W = Q_LORA + KV_LORA + LANES
LOC_W = POOL_W + 2 * SGU_W + 3 * CONV_W
HALO = 16
N_MOD = 6
MOD_ROWS = 16
VMEM_LIMIT = 52 * 1024 * 1024


def _cparams(*sem):
    return pltpu.CompilerParams(dimension_semantics=sem, vmem_limit_bytes=VMEM_LIMIT)


def _rope_perm():
    j = np.arange(QK_ROPE)
    first_half = (j % (2 * AXIS_FREQS)) < AXIS_FREQS
    perm = np.where(first_half, j + AXIS_FREQS, j - AXIS_FREQS)
    sign = np.where(first_half, -1.0, 1.0).astype(np.float32)
    return perm, sign


def _ada_kernel(c_ref, w_ref, b_ref, o_ref):
    c = c_ref[...]
    s = c * jax.nn.sigmoid(c)
    s_hi = s.astype(BF16)
    s_lo = (s - s_hi.astype(F32)).astype(BF16)
    w = w_ref[...]
    w_hi = w.astype(BF16)
    w_lo = (w - w_hi.astype(F32)).astype(BF16)
    acc = jnp.dot(s_hi, w_hi, preferred_element_type=F32)
    acc += jnp.dot(s_lo, w_hi, preferred_element_type=F32)
    acc += jnp.dot(s_hi, w_lo, preferred_element_type=F32)
    o_ref[...] = acc + b_ref[...]


def _ada_params(cond, w_mod, b_mod):
    L = w_mod.shape[0]
    tn = 1024
    return pl.pallas_call(
        _ada_kernel,
        grid=(L, N_MOD * D_MODEL // tn),
        in_specs=[
            pl.BlockSpec((MOD_ROWS, D_MODEL), lambda l, j: (0, 0)),
            pl.BlockSpec((None, D_MODEL, tn), lambda l, j: (l, 0, j)),
            pl.BlockSpec((None, 1, tn), lambda l, j: (l, 0, j)),
        ],
        out_specs=pl.BlockSpec((None, MOD_ROWS, tn), lambda l, j: (l, 0, j)),
        out_shape=jax.ShapeDtypeStruct((L, MOD_ROWS, N_MOD * D_MODEL), F32),
        compiler_params=_cparams("arbitrary", "arbitrary"),
        name="ada_params",
    )(cond, w_mod, b_mod)


def _rms(x, w):
    return x * lax.rsqrt(jnp.mean(x * x, axis=-1, keepdims=True) + EPS) * w


def _modulated(x, nw, shift, scale):
    return _rms(x, nw) * (1.0 + scale) + shift


class _Geom:
    def __init__(self, batch, seq, dec_batch, dec_seq):
        self.seq, self.dec_seq = seq, dec_seq
        self.m_ctx = batch * seq
        self.m_lat = dec_batch * dec_seq
        self.m = self.m_ctx + self.m_lat

    def mod_row(self, t0):
        return jnp.where(t0 < self.m_ctx, 0, 1 + (t0 - self.m_ctx) // self.dec_seq)

    def tile(self, want):
        t = min(want, self.seq, self.dec_seq)
        assert self.m_ctx % t == 0 and self.m_lat % t == 0 and self.seq % t == 0 and self.dec_seq % t == 0
        return t

    def tile_tokenwise(self, want):
        t = want
        while self.m_ctx % t or self.dec_seq % t:
            t //= 2
        return t


def _mod_spec(layer, which, geom, tm):
    def imap(i, *_):
        return ((layer * MOD_ROWS + geom.mod_row(i * tm)) * N_MOD + which, 0, 0)
    return pl.BlockSpec((None, 1, D_MODEL), imap)


def _const_spec(shape):
    nd = len(shape)
    return pl.BlockSpec(shape, lambda *_: (0,) * nd)


def _modulate_kernel(x_ref, nw_ref, sh_ref, sc_ref, h_ref):
    h_ref[...] = _modulated(x_ref[...], nw_ref[...], sh_ref[...], sc_ref[...]).astype(BF16)


def _modulate(x, mods, norm_w, layer, geom):
    tm = geom.tile_tokenwise(512)
    return pl.pallas_call(
        _modulate_kernel,
        grid=(geom.m // tm,),
        in_specs=[
            pl.BlockSpec((tm, D_MODEL), lambda i: (i, 0)),
            _const_spec((1, D_MODEL)),
            _mod_spec(layer, 0, geom, tm),
            _mod_spec(layer, 1, geom, tm),
        ],
        out_specs=pl.BlockSpec((tm, D_MODEL), lambda i: (i, 0)),
        out_shape=jax.ShapeDtypeStruct((geom.m, D_MODEL), BF16),
        compiler_params=_cparams("arbitrary"),
        name="modulate",
    )(x, norm_w, mods, mods)


def _lane_iota(rows):
    return lax.broadcasted_iota(jnp.int32, (rows, LANES), 1)


def _keys_values(ckv_bf, kpe_blk, tab, w_uk_ref, w_uv_ref, kvn_ref, kvp_ref, k_ref, v_ref):
    rows = ckv_bf.shape[0]
    lane = _lane_iota(rows)
    rope_lanes = (lane >= QK_NOPE) & (lane < QK_HEAD)
    v_ref[...] = jnp.dot(ckv_bf, w_uv_ref[...], preferred_element_type=F32).astype(BF16)
    ssq_pe = jnp.sum(jnp.where(rope_lanes, kpe_blk * kpe_blk, 0.0), axis=-1, keepdims=True)
    t = kpe_blk * (kvp_ref[...] * tab)
    pe = jnp.where(rope_lanes, t + pltpu.roll(t, LANES - QK_ROPE, 1), 0.0)
    kn = jnp.dot(ckv_bf, w_uk_ref[...], preferred_element_type=F32)
    kvn = kvn_ref[...]
    for hd in range(N_HEADS):
        sl = slice(hd * HEAD_PAD, (hd + 1) * HEAD_PAD)
        knh = kn[:, sl]
        ssq = jnp.sum(knh * knh, axis=-1, keepdims=True) + ssq_pe
        s = lax.rsqrt(ssq * (1.0 / QK_HEAD) + EPS)
        k_ref[:, sl] = (s * (knh * kvn + pe)).astype(BF16)


def _mla_kernel(h_ref, tab_ref, w_mla_ref, qa_ref, kva_ref, w_uq_ref, qv_ref, w_uk_ref, w_uv_ref,
                kvn_ref, kvp_ref, q_ref, k_ref, v_ref, ckv_ref, kpe_ref):
    z = jnp.dot(h_ref[...], w_mla_ref[...], preferred_element_type=F32)
    c_q = _rms(z[:, :OFF_CKV], qa_ref[...]).astype(BF16)
    c_kv = _rms(z[:, OFF_CKV:OFF_KPE], kva_ref[...])
    kpe_blk = z[:, OFF_KPE:]
    ckv_ref[...] = c_kv
    kpe_ref[...] = kpe_blk
    tab = tab_ref[...]
    rows = z.shape[0]
    lane = _lane_iota(rows)
    qraw = jnp.dot(c_q, w_uq_ref[...], preferred_element_type=F32)
    qt = qv_ref[...] * tab
    for hd in range(N_HEADS):
        sl = slice(hd * HEAD_PAD, (hd + 1) * HEAD_PAD)
        qh = qraw[:, sl]
        ssq = jnp.sum(jnp.where(lane < QK_HEAD, qh * qh, 0.0), axis=-1, keepdims=True)
        t = qh * (lax.rsqrt(ssq * (1.0 / QK_HEAD) + EPS) * qt)
        r = pltpu.roll(t, LANES - QK_ROPE, 1)
        q_ref[:, sl] = jnp.where(lane < QK_NOPE, t, jnp.where(lane < QK_HEAD, t + r, 0.0)).astype(BF16)
    _keys_values(c_kv.astype(BF16), kpe_blk, tab, w_uk_ref, w_uv_ref, kvn_ref, kvp_ref, k_ref, v_ref)


def _mla_proj(h, tab_all, lw, geom):
    tm = geom.tile_tokenwise(256)
    n_tab = geom.dec_seq // tm

    def tab_map(i):
        t0 = i * tm
        return (jnp.where(t0 < geom.m_ctx, 0, 1 + ((t0 - geom.m_ctx) // tm) % n_tab), 0)

    row = lambda w: pl.BlockSpec((tm, w), lambda i: (i, 0))
    return pl.pallas_call(
        _mla_kernel,
        grid=(geom.m // tm,),
        in_specs=[
            row(D_MODEL),
            pl.BlockSpec((tm, LANES), tab_map),
            _const_spec((D_MODEL, MLA_W)),
            _const_spec((1, Q_LORA)),
            _const_spec((1, KV_LORA)),
            _const_spec((Q_LORA, QK_PAD)),
            _const_spec((1, LANES)),
            _const_spec((KV_LORA, QK_PAD)),
            _const_spec((KV_LORA, ATTN_W)),
            _const_spec((1, LANES)),
            _const_spec((1, LANES)),
        ],
        out_specs=[row(QK_PAD), row(QK_PAD), row(ATTN_W), row(KV_LORA), row(LANES)],
        out_shape=[
            jax.ShapeDtypeStruct((geom.m, QK_PAD), BF16),
            jax.ShapeDtypeStruct((geom.m, QK_PAD), BF16),
            jax.ShapeDtypeStruct((geom.m, ATTN_W), BF16),
            jax.ShapeDtypeStruct((geom.m, KV_LORA), F32),
            jax.ShapeDtypeStruct((geom.m, LANES), F32),
        ],
        compiler_params=_cparams("arbitrary"),
        name="mla_proj",
    )(h, tab_all, lw["w_mla"], lw["qa_w"], lw["kva_w"], lw["w_uq"], lw["qvec"], lw["w_uk"], lw["w_uv"],
      lw["kvec_nope"], lw["kvec_pe"])


def _cache_kv_kernel(ckv_ref, kpe_ref, tab_ref, w_uk_ref, w_uv_ref, kvn_ref, kvp_ref, k_ref, v_ref):
    _keys_values(ckv_ref[...].astype(BF16), kpe_ref[...], tab_ref[...], w_uk_ref, w_uv_ref,
                 kvn_ref, kvp_ref, k_ref, v_ref)


def _cache_kv(ckv, kpe_blk, tab_all, lw):
    rows = ckv.shape[0]
    tm = min(256, rows)
    assert rows % tm == 0 and tab_all.shape[0] >= tm
    row = lambda w: pl.BlockSpec((tm, w), lambda i: (i, 0))
    return pl.pallas_call(
        _cache_kv_kernel,
        grid=(rows // tm,),
        in_specs=[
            row(KV_LORA), row(LANES),
            pl.BlockSpec((tm, LANES), lambda i: (0, 0)),
            _const_spec((KV_LORA, QK_PAD)),
            _const_spec((KV_LORA, ATTN_W)),
            _const_spec((1, LANES)),
            _const_spec((1, LANES)),
        ],
        out_specs=[row(QK_PAD), row(ATTN_W)],
        out_shape=[
            jax.ShapeDtypeStruct((rows, QK_PAD), BF16),
            jax.ShapeDtypeStruct((rows, ATTN_W), BF16),
        ],
        compiler_params=_cparams("arbitrary"),
        name="cache_kv",
    )(ckv, kpe_blk, tab_all, lw["w_uk"], lw["w_uv"], lw["kvec_nope"], lw["kvec_pe"])


HEADS_PER_STEP = 2


def _attn_kernel(*refs, has_cache):
    if has_cache:
        q_ref, k_ref, v_ref, kc_ref, vc_ref, o_ref = refs
    else:
        q_ref, k_ref, v_ref, o_ref = refs
    nt = (((1,), (1,)), ((), ()))
    outs = []
    for hh in range(HEADS_PER_STEP):
        qs = slice(hh * HEAD_PAD, (hh + 1) * HEAD_PAD)
        vs = slice(hh * V_HEAD, (hh + 1) * V_HEAD)
        q = q_ref[:, qs]
        s = lax.dot_general(q, k_ref[:, qs], nt, preferred_element_type=F32)
        m = jnp.max(s, axis=-1, keepdims=True)
        if has_cache:
            sc = lax.dot_general(q, kc_ref[:, qs], nt, preferred_element_type=F32)
            m = jnp.maximum(m, jnp.max(sc, axis=-1, keepdims=True))
        p = jnp.exp(s - m)
        l = jnp.sum(p, axis=-1, keepdims=True)
        o = jnp.dot(p.astype(BF16), v_ref[:, vs], preferred_element_type=F32)
        if has_cache:
            pc = jnp.exp(sc - m)
            l = l + jnp.sum(pc, axis=-1, keepdims=True)
            o = o + jnp.dot(pc.astype(BF16), vc_ref[:, vs], preferred_element_type=F32)
        outs.append(o / l)
    o_ref[...] = jnp.concatenate(outs, axis=-1).astype(BF16)


def _attention(q, k, v, row0, n_seq, seq_len, tq, cache=None):
    assert row0 % seq_len == 0 and seq_len % tq == 0
    nq = seq_len // tq
    hp = N_HEADS // HEADS_PER_STEP
    qb0, sb0 = row0 // tq, row0 // seq_len
    kw, vw = HEADS_PER_STEP * HEAD_PAD, HEADS_PER_STEP * V_HEAD
    in_specs = [
        pl.BlockSpec((tq, kw), lambda b, h, i: (qb0 + b * nq + i, h)),
        pl.BlockSpec((seq_len, kw), lambda b, h, i: (sb0 + b, h)),
        pl.BlockSpec((seq_len, vw), lambda b, h, i: (sb0 + b, h)),
    ]
    args = [q, k, v]
    if cache is not None:
        k_c, v_c, past = cache
        in_specs += [
            pl.BlockSpec((past, kw), lambda b, h, i: (b, h)),
            pl.BlockSpec((past, vw), lambda b, h, i: (b, h)),
        ]
        args += [k_c, v_c]
    return pl.pallas_call(
        functools.partial(_attn_kernel, has_cache=cache is not None),
        grid=(n_seq, hp, nq),
        in_specs=in_specs,
        out_specs=pl.BlockSpec((tq, vw), lambda b, h, i: (b * nq + i, h)),
        out_shape=jax.ShapeDtypeStruct((n_seq * seq_len, ATTN_W), BF16),
        compiler_params=_cparams("arbitrary", "arbitrary", "arbitrary"),
        name="attention_cache" if cache is not None else "attention",
    )(*args)


def _shift_rows(x, d):
    n = x.shape[0]
    return pltpu.roll(x, (-d) % n, 0)


def _local_kernel(hp_ref, h_ref, hn_ref, w_ref, wpool_ref, pscale_ref, sgw_ref, wsp_ref, bsp_ref, cw_ref,
                  o_ref, *, tm, geom):
    i = pl.program_id(0)
    t0 = i * tm
    in_ctx = t0 < geom.m_ctx
    seq = jnp.where(in_ctx, geom.seq, geom.dec_seq)
    pos0 = jnp.where(in_ctx, t0 % geom.seq, (t0 - geom.m_ctx) % geom.dec_seq)
    rows = tm + 2 * HALO
    hext = jnp.concatenate([hp_ref[...], h_ref[...], hn_ref[...]], axis=0)
    z = jnp.dot(hext, w_ref[...], preferred_element_type=F32)
    p_ext = pos0 - HALO + lax.broadcasted_iota(jnp.int32, (rows, 1), 0)
    valid = (p_ext >= 0) & (p_ext < seq)
    inner = slice(HALO, HALO + tm)
    p = p_ext[inner]

    for g, w in enumerate(POOL_WINDOWS):
        x = jnp.where(valid, z[:, g * POOL_GROUP_W:(g + 1) * POOL_GROUP_W], 0.0)
        a, span = x, 1
        while span < w:
            a = a + _shift_rows(a, span)
            span *= 2
        win = _shift_rows(a, -(w // 2))[inner]
        lo = jnp.clip(p - w // 2, 0, seq - 1)
        hi = jnp.clip(p + w // 2 - 1, 0, seq - 1)
        cnt = (hi - lo + 1).astype(F32)
        pooled = win / cnt - x[inner]
        po = jnp.dot(pooled.astype(BF16), wpool_ref[g], preferred_element_type=F32)
        gs = slice(g * POOL_GROUP_W, (g + 1) * POOL_GROUP_W)
        o_ref[:, gs] = (po * pscale_ref[:, gs]).astype(BF16)

    zs = jax.nn.gelu(z[inner, POOL_W:POOL_W + 2 * SGU_W])
    u = zs[:, :SGU_W]
    vn = _rms(zs[:, SGU_W:], sgw_ref[...]).astype(BF16)
    gw = SGU_W // SGU_GROUPS
    for n in range(tm // CHUNK):
        rs = slice(n * CHUNK, (n + 1) * CHUNK)
        for g in range(SGU_GROUPS):
            cs = slice(g * gw, (g + 1) * gw)
            mixed = jnp.dot(wsp_ref[g], vn[rs, cs], preferred_element_type=F32) + bsp_ref[g]
            o_ref[rs, POOL_W + g * gw:POOL_W + (g + 1) * gw] = (u[rs, cs] * mixed).astype(BF16)

    c0 = POOL_W + 2 * SGU_W
    zc = z[:, c0:]
    y = jnp.where(valid, zc[:, CONV_W:2 * CONV_W] * zc[:, 2 * CONV_W:], 0.0)
    conv = (cw_ref[0:1, :] * _shift_rows(y, -1)[inner] + cw_ref[1:2, :] * y[inner]
            + cw_ref[2:3, :] * _shift_rows(y, 1)[inner])
    o_ref[:, POOL_W + SGU_W:] = (zc[inner, :CONV_W] * conv).astype(BF16)


def _local_branches(h, lw, geom):
    tm = geom.tile(256)
    assert tm % CHUNK == 0 and tm % HALO == 0
    nb = tm // HALO
    last = geom.m // HALO - 1
    return pl.pallas_call(
        functools.partial(_local_kernel, tm=tm, geom=geom),
        grid=(geom.m // tm,),
        in_specs=[
            pl.BlockSpec((HALO, D_MODEL), lambda i: (jnp.maximum(i * nb - 1, 0), 0)),
            pl.BlockSpec((tm, D_MODEL), lambda i: (i, 0)),
            pl.BlockSpec((HALO, D_MODEL), lambda i: (jnp.minimum((i + 1) * nb, last), 0)),
            _const_spec((D_MODEL, LOC_W)),
            _const_spec((len(POOL_WINDOWS), POOL_GROUP_W, POOL_GROUP_W)),
            _const_spec((1, POOL_W)),
            _const_spec((1, SGU_W)),
            _const_spec((SGU_GROUPS, CHUNK, CHUNK)),
            _const_spec((SGU_GROUPS, CHUNK, SGU_W // SGU_GROUPS)),
            _const_spec((3, CONV_W)),
        ],
        out_specs=pl.BlockSpec((tm, POOL_W + SGU_W + CONV_W), lambda i: (i, 0)),
        out_shape=jax.ShapeDtypeStruct((geom.m, POOL_W + SGU_W + CONV_W), BF16),
        compiler_params=_cparams("arbitrary"),
        name="local_branches",
    )(h, h, h, lw["w_loc"], lw["w_pool"], lw["pool_scale"], lw["sgu_norm_w"], lw["w_spatial"],
      lw["b_spatial"], lw["conv_w"])


def _merge_kernel(h_ref, a_ref, br_ref, wg0, wg1, wg2, wg3, bg0, bg1, bg2, bg3, wa_ref, wp_ref, ws_ref, wc_ref,
                  o_ref):
    h = h_ref[...]
    branches = (
        (a_ref[...], wa_ref, wg0, bg0),
        (br_ref[:, :POOL_W], wp_ref, wg1, bg1),
        (br_ref[:, POOL_W:POOL_W + SGU_W], ws_ref, wg2, bg2),
        (br_ref[:, POOL_W + SGU_W:], wc_ref, wg3, bg3),
    )
    acc = None
    for x, w_ref, wg_ref, bg_ref in branches:
        g = jax.nn.sigmoid(jnp.dot(h, wg_ref[...], preferred_element_type=F32) + bg_ref[...])
        t = g * jnp.dot(x, w_ref[...], preferred_element_type=F32)
        acc = t if acc is None else acc + t
    o_ref[...] = acc.astype(BF16)


def _merge(h, attn_o, br, lw, geom):
    tm = geom.tile_tokenwise(1024)
    tn = 256
    nn = D_MODEL // tn
    gate_specs = [pl.BlockSpec((D_MODEL, tn), functools.partial(lambda b, i, j: (0, b * nn + j), b))
                  for b in range(N_BRANCH)]
    bias_specs = [pl.BlockSpec((1, tn), functools.partial(lambda b, i, j: (0, b * nn + j), b))
                  for b in range(N_BRANCH)]
    col = lambda k: pl.BlockSpec((k, tn), lambda i, j: (0, j))
    return pl.pallas_call(
        _merge_kernel,
        grid=(geom.m // tm, nn),
        in_specs=[
            pl.BlockSpec((tm, D_MODEL), lambda i, j: (i, 0)),
            pl.BlockSpec((tm, ATTN_W), lambda i, j: (i, 0)),
            pl.BlockSpec((tm, POOL_W + SGU_W + CONV_W), lambda i, j: (i, 0)),
            *gate_specs, *bias_specs,
            col(ATTN_W), col(POOL_W), col(SGU_W), col(CONV_W),
        ],
        out_specs=pl.BlockSpec((tm, tn), lambda i, j: (i, j)),
        out_shape=jax.ShapeDtypeStruct((geom.m, D_MODEL), BF16),
        compiler_params=_cparams("arbitrary", "arbitrary"),
        name="merge",
    )(h, attn_o, br, *([lw["w_gate"]] * N_BRANCH), *([lw["b_gate"]] * N_BRANCH),
      lw["w_br_attn"], lw["w_br_pool"], lw["w_br_sgu"], lw["w_br_conv"])


def _outproj_kernel(m_ref, w_ref, x_ref, ga_ref, nw_ref, sh_ref, sc_ref, x1_ref, h2_ref):
    mix = jnp.dot(m_ref[...], w_ref[...], preferred_element_type=F32)
    x1 = x_ref[...] + ga_ref[...] * mix
    x1_ref[...] = x1
    h2_ref[...] = _modulated(x1, nw_ref[...], sh_ref[...], sc_ref[...]).astype(BF16)


def _outproj(m, x, mods, lw, layer, geom):
    tm = geom.tile_tokenwise(256)
    row = lambda: pl.BlockSpec((tm, D_MODEL), lambda i: (i, 0))
    return pl.pallas_call(
        _outproj_kernel,
        grid=(geom.m // tm,),
        in_specs=[
            row(), _const_spec((D_MODEL, D_MODEL)), row(),
            _mod_spec(layer, 2, geom, tm),
            _const_spec((1, D_MODEL)),
            _mod_spec(layer, 3, geom, tm),
            _mod_spec(layer, 4, geom, tm),
        ],
        out_specs=[row(), row()],
        out_shape=[
            jax.ShapeDtypeStruct((geom.m, D_MODEL), F32),
            jax.ShapeDtypeStruct((geom.m, D_MODEL), BF16),
        ],
        compiler_params=_cparams("arbitrary"),
        name="outproj",
    )(m, lw["w_out"], x, mods, lw["norm_ffn_w"], mods, mods)


def _ffn_in_kernel(h_ref, wg_ref, wu_ref, o_ref):
    h = h_ref[...]
    a = jnp.dot(h, wg_ref[...], preferred_element_type=F32)
    b = jnp.dot(h, wu_ref[...], preferred_element_type=F32)
    o_ref[...] = (a * jax.nn.sigmoid(a) * b).astype(BF16)


def _ffn_in(h2, lw, geom):
    tm = geom.tile_tokenwise(1024)
    tf = 512
    nf = D_FF // tf
    return pl.pallas_call(
        _ffn_in_kernel,
        grid=(geom.m // tm, nf),
        in_specs=[
            pl.BlockSpec((tm, D_MODEL), lambda i, j: (i, 0)),
            pl.BlockSpec((D_MODEL, tf), lambda i, j: (0, j)),
            pl.BlockSpec((D_MODEL, tf), lambda i, j: (0, nf + j)),
        ],
        out_specs=pl.BlockSpec((tm, tf), lambda i, j: (i, j)),
        out_shape=jax.ShapeDtypeStruct((geom.m, D_FF), BF16),
        compiler_params=_cparams("arbitrary", "arbitrary"),
        name="ffn_in",
    )(h2, lw["w_ffn_in"], lw["w_ffn_in"])


def _ffn_out_kernel(a_ref, w_ref, x_ref, g_ref, o_ref):
    y = jnp.dot(a_ref[...], w_ref[...], preferred_element_type=F32)
    o_ref[...] = x_ref[...] + g_ref[...] * y


def _ffn_out(act, x1, mods, lw, layer, geom):
    tm = geom.tile_tokenwise(512)
    tn = 512

    def g_map(i, j):
        return ((layer * MOD_ROWS + geom.mod_row(i * tm)) * N_MOD + 5, 0, j)

    return pl.pallas_call(
        _ffn_out_kernel,
        grid=(geom.m // tm, D_MODEL // tn),
        in_specs=[
            pl.BlockSpec((tm, D_FF), lambda i, j: (i, 0)),
            pl.BlockSpec((D_FF, tn), lambda i, j: (0, j)),
            pl.BlockSpec((tm, tn), lambda i, j: (i, j)),
            pl.BlockSpec((None, 1, tn), g_map),
        ],
        out_specs=pl.BlockSpec((tm, tn), lambda i, j: (i, j)),
        out_shape=jax.ShapeDtypeStruct((geom.m, D_MODEL), F32),
        compiler_params=_cparams("arbitrary", "arbitrary"),
        name="ffn_out",
    )(act, lw["w_ffn_out"], x1, mods)


def _rope_table(dec_seq, ident_rows):
    t = np.arange(dec_seq)
    row = (t // GRID_W).astype(np.float32)
    col = (t % GRID_W).astype(np.float32)
    inv = jnp.asarray(ROPE_THETA, F32) ** (-jnp.arange(AXIS_FREQS, dtype=F32) / AXIS_FREQS)
    ang_r = jnp.asarray(row)[:, None] * inv
    ang_c = jnp.asarray(col)[:, None] * inv
    ang = jnp.concatenate([ang_r, ang_r, ang_c, ang_c], axis=1)
    lat = jnp.concatenate([jnp.ones((dec_seq, QK_NOPE), F32), jnp.cos(ang), jnp.sin(ang)], axis=1)
    ident = jnp.concatenate([jnp.ones((ident_rows, QK_HEAD), F32), jnp.zeros((ident_rows, QK_ROPE), F32)], axis=1)
    return jnp.concatenate([ident, lat], axis=0)


def _prep_layer(l, w_in, b_gate, q_a_norm_w, kv_a_norm_w, w_uq, w_ukv, q_norm_w, k_norm_w, w_pool, pool_scale,
                sgu_norm_w, w_spatial, b_spatial, conv_w, w_br_attn, w_br_pool, w_br_sgu, w_br_conv, w_out,
                w_ffn_in, w_ffn_out, norm_ffn_w):
    perm, sign = _rope_perm()
    wi = w_in[l]
    w_kpe = wi[:, OFF_KPE:OFF_POOL]
    w_mla = jnp.concatenate(
        [wi[:, :OFF_KPE], jnp.zeros((D_MODEL, QK_NOPE), F32), w_kpe, w_kpe[:, perm] * sign], axis=1)
    uq = w_uq[l].reshape(Q_LORA, N_HEADS, QK_HEAD)
    uq_r = uq[:, :, QK_NOPE:]
    uq = jnp.concatenate([uq, uq_r[:, :, perm] * sign], axis=-1).reshape(Q_LORA, QK_PAD)
    qn = q_norm_w[l]
    qvec = (jnp.concatenate([qn, qn[QK_NOPE:][perm]]) * ATTN_SCALE).reshape(1, LANES)
    ukv = w_ukv[l].reshape(KV_LORA, N_HEADS, QK_NOPE + V_HEAD)
    uk = jnp.concatenate([ukv[:, :, :QK_NOPE], jnp.zeros((KV_LORA, N_HEADS, HEAD_PAD - QK_NOPE), F32)], axis=-1)
    kn = k_norm_w[l]
    zeros_nope = jnp.zeros((QK_NOPE,), F32)
    bsp = jnp.broadcast_to(b_spatial[l][:, :, None], (SGU_GROUPS, CHUNK, SGU_W // SGU_GROUPS))
    return dict(
        w_mla=w_mla.astype(BF16),
        qa_w=q_a_norm_w[l].reshape(1, Q_LORA),
        kva_w=kv_a_norm_w[l].reshape(1, KV_LORA),
        w_uq=uq.astype(BF16),
        qvec=qvec,
        w_uk=uk.reshape(KV_LORA, QK_PAD).astype(BF16),
        w_uv=ukv[:, :, QK_NOPE:].reshape(KV_LORA, ATTN_W).astype(BF16),
        kvec_nope=jnp.concatenate([kn[:QK_NOPE], zeros_nope]).reshape(1, LANES),
        kvec_pe=jnp.concatenate([zeros_nope, kn[QK_NOPE:], kn[QK_NOPE:][perm]]).reshape(1, LANES),
        w_loc=wi[:, OFF_POOL:OFF_GATE].astype(BF16),
        w_gate=wi[:, OFF_GATE:].astype(BF16),
        b_gate=b_gate[l].reshape(1, N_BRANCH * D_MODEL),
        w_pool=w_pool[l].astype(BF16),
        pool_scale=pool_scale[l].reshape(1, POOL_W),
        sgu_norm_w=sgu_norm_w[l].reshape(1, SGU_W),
        w_spatial=w_spatial[l].astype(BF16),
        b_spatial=bsp,
        conv_w=conv_w[l],
        w_br_attn=w_br_attn[l].astype(BF16),
        w_br_pool=w_br_pool[l].astype(BF16),
        w_br_sgu=w_br_sgu[l].astype(BF16),
        w_br_conv=w_br_conv[l].astype(BF16),
        w_out=w_out[l].astype(BF16),
        w_ffn_in=w_ffn_in[l].astype(BF16),
        w_ffn_out=w_ffn_out[l].astype(BF16),
        norm_ffn_w=norm_ffn_w[l].reshape(1, D_MODEL),
    )


def kernel(x_prompt, x_sample, cache_ckv, cache_kpe, c, c_ctx, w_mod, b_mod, norm_mix_w, norm_ffn_w, w_in, b_gate, q_a_norm_w, kv_a_norm_w, w_uq, w_ukv, q_norm_w, k_norm_w, w_pool, pool_scale, sgu_norm_w, w_spatial, b_spatial, conv_w, w_br_attn, w_br_pool, w_br_sgu, w_br_conv, w_out, w_ffn_in, w_ffn_out):
    batch, seq, _ = x_prompt.shape
    dec_batch, dec_seq, _ = x_sample.shape
    depth = w_in.shape[0]
    past = cache_ckv.shape[2]
    geom = _Geom(batch, seq, dec_batch, dec_seq)
    assert 1 + dec_batch <= MOD_ROWS

    cond = jnp.concatenate([c_ctx[None, :], c, jnp.zeros((MOD_ROWS - 1 - dec_batch, D_MODEL), F32)], axis=0)
    mods = _ada_params(cond, w_mod, b_mod.reshape(depth, 1, N_MOD * D_MODEL))
    mods = mods.reshape(depth * MOD_ROWS * N_MOD, 1, D_MODEL)

    tab_all = _rope_table(dec_seq, geom.tile_tokenwise(256))
    x = jnp.concatenate([x_prompt.reshape(geom.m_ctx, D_MODEL), x_sample.reshape(geom.m_lat, D_MODEL)], axis=0)
    tq_lat = min(256, dec_seq)
    tq_ctx = min(256, seq)

    ckv_states, kpe_states = [], []
    for l in range(depth):
        lw = _prep_layer(l, w_in, b_gate, q_a_norm_w, kv_a_norm_w, w_uq, w_ukv, q_norm_w, k_norm_w, w_pool,
                         pool_scale, sgu_norm_w, w_spatial, b_spatial, conv_w, w_br_attn, w_br_pool, w_br_sgu,
                         w_br_conv, w_out, w_ffn_in, w_ffn_out, norm_ffn_w)
        h = _modulate(x, mods, norm_mix_w[l].reshape(1, D_MODEL), l, geom)
        q, k, v, ckv, kpe = _mla_proj(h, tab_all, lw, geom)
        cache_kpe_blk = jnp.pad(cache_kpe[:, l].reshape(dec_batch * past, QK_ROPE),
                                ((0, 0), (QK_NOPE, LANES - QK_HEAD)))
        k_c, v_c = _cache_kv(cache_ckv[:, l].reshape(dec_batch * past, KV_LORA), cache_kpe_blk, tab_all, lw)
        attn_ctx = _attention(q, k, v, 0, batch, seq, tq_ctx)
        attn_lat = _attention(q, k, v, geom.m_ctx, dec_batch, dec_seq, tq_lat, cache=(k_c, v_c, past))
        attn_o = jnp.concatenate([attn_ctx, attn_lat], axis=0)
        br = _local_branches(h, lw, geom)
        m = _merge(h, attn_o, br, lw, geom)
        x1, h2 = _outproj(m, x, mods, lw, l, geom)
        act = _ffn_in(h2, lw, geom)
        x = _ffn_out(act, x1, mods, lw, l, geom)
        ckv_states.append(ckv[:geom.m_ctx].reshape(batch, seq, KV_LORA))
        kpe_states.append(kpe[:geom.m_ctx, QK_NOPE:QK_HEAD].reshape(batch, seq, QK_ROPE))

    y_prompt = x[:geom.m_ctx].reshape(batch, seq, D_MODEL)
    y_sample = x[geom.m_ctx:].reshape(dec_batch, dec_seq, D_MODEL)
    return (y_prompt, y_sample, jnp.stack(ckv_states, axis=1), jnp.stack(kpe_states, axis=1))
```

```python
import functools

import numpy as np
import jax
import jax.numpy as jnp
from jax import lax
from jax.experimental import pallas as pl
from jax.experimental.pallas import tpu as pltpu

F32 = jnp.float32
BF16 = jnp.bfloat16

D_MODEL = 2048
EPS = 1e-6
GRID_W = 64
N_HEADS = 16
QK_NOPE = 64
QK_ROPE = 32
QK_HEAD = QK_NOPE + QK_ROPE
V_HEAD = 64
Q_LORA = 768
KV_LORA = 256
ROPE_THETA = 10000.0
AXIS_FREQS = QK_ROPE // 4
ATTN_SCALE = QK_HEAD ** -0.5
ATTN_W = N_HEADS * V_HEAD
POOL_WINDOWS = (2, 4, 8, 16)
POOL_GROUP_W = 128
POOL_W = POOL_GROUP_W * len(POOL_WINDOWS)
CHUNK = 128
SGU_GROUPS = 4
SGU_W = 512
CONV_W = 512
N_BRANCH = 4
D_FF = 5632
OFF_CKV = Q_LORA
OFF_KPE = OFF_CKV + KV_LORA
OFF_POOL = OFF_KPE + QK_ROPE
OFF_SGU = OFF_POOL + POOL_W
OFF_CONV = OFF_SGU + 2 * SGU_W
OFF_GATE = OFF_CONV + 3 * CONV_W

LANES = 128
HEAD_PAD = LANES
QK_PAD = N_HEADS * HEAD_PAD
MLA_W = Q_LORA + KV_LORA + LANES
LOC_W = POOL_W + 2 * SGU_W + 3 * CONV_W
BF16_ROWS = 16
VT_HEAD = V_HEAD + BF16_ROWS
VT_ROWS = N_HEADS * VT_HEAD
HALO = BF16_ROWS
LOG2E = 1.4426950408889634
_NT = (((1,), (1,)), ((), ()))
N_MOD = 6
MOD_ROWS = 16
VMEM_LIMIT = 52 * 1024 * 1024


def _cparams(*sem):
    return pltpu.CompilerParams(dimension_semantics=sem, vmem_limit_bytes=VMEM_LIMIT)


def _rope_perm():
    j = np.arange(QK_ROPE)
    first_half = (j % (2 * AXIS_FREQS)) < AXIS_FREQS
    perm = np.where(first_half, j + AXIS_FREQS, j - AXIS_FREQS)
    sign = np.where(first_half, -1.0, 1.0).astype(np.float32)
    return perm, sign


def _ada_kernel(c_ref, w_ref, b_ref, o_ref):
    c = c_ref[...]
    s = c * jax.nn.sigmoid(c)
    s_hi = s.astype(BF16)
    s_lo = (s - s_hi.astype(F32)).astype(BF16)
    w = w_ref[...]
    w_hi = w.astype(BF16)
    w_lo = (w - w_hi.astype(F32)).astype(BF16)
    acc = jnp.dot(s_hi, w_hi, preferred_element_type=F32)
    acc += jnp.dot(s_lo, w_hi, preferred_element_type=F32)
    acc += jnp.dot(s_hi, w_lo, preferred_element_type=F32)
    o_ref[...] = acc + b_ref[...]


def _ada_params(cond, w_mod, b_mod):
    L = w_mod.shape[0]
    tn = 1024
    return pl.pallas_call(
        _ada_kernel,
        grid=(L, N_MOD * D_MODEL // tn),
        in_specs=[
            pl.BlockSpec((MOD_ROWS, D_MODEL), lambda l, j: (0, 0)),
            pl.BlockSpec((None, D_MODEL, tn), lambda l, j: (l, 0, j)),
            pl.BlockSpec((None, 1, tn), lambda l, j: (l, 0, j)),
        ],
        out_specs=pl.BlockSpec((None, MOD_ROWS, tn), lambda l, j: (l, 0, j)),
        out_shape=jax.ShapeDtypeStruct((L, MOD_ROWS, N_MOD * D_MODEL), F32),
        compiler_params=_cparams("arbitrary", "arbitrary"),
        name="ada_params",
    )(cond, w_mod, b_mod)


def _rms(x, w):
    return x * lax.rsqrt(jnp.mean(x * x, axis=-1, keepdims=True) + EPS) * w


def _modulated(x, nw, shift, scale):
    return _rms(x, nw) * (1.0 + scale) + shift


class _Geom:
    def __init__(self, batch, seq, dec_batch, dec_seq):
        self.seq, self.dec_seq = seq, dec_seq
        self.m_ctx = batch * seq
        self.m_lat = dec_batch * dec_seq
        self.m = self.m_ctx + self.m_lat

    def mod_row(self, t0):
        return jnp.where(t0 < self.m_ctx, 0, 1 + (t0 - self.m_ctx) // self.dec_seq)

    def tile(self, want):
        t = min(want, self.seq, self.dec_seq)
        assert self.m_ctx % t == 0 and self.m_lat % t == 0 and self.seq % t == 0 and self.dec_seq % t == 0
        return t

    def tile_tokenwise(self, want):
        t = want
        while self.m_ctx % t or self.dec_seq % t:
            t //= 2
        return t


def _mod_spec(layer, which, geom, tm):
    def imap(i, *_):
        return ((layer * MOD_ROWS + geom.mod_row(i * tm)) * N_MOD + which, 0, 0)
    return pl.BlockSpec((None, 1, D_MODEL), imap)


def _const_spec(shape):
    nd = len(shape)
    return pl.BlockSpec(shape, lambda *_: (0,) * nd)


def _modulate_kernel(x_ref, nw_ref, sh_ref, sc_ref, h_ref):
    h_ref[...] = _modulated(x_ref[...], nw_ref[...], sh_ref[...], sc_ref[...]).astype(BF16)


def _modulate(x, mods, norm_w, layer, geom):
    tm = geom.tile_tokenwise(512)
    return pl.pallas_call(
        _modulate_kernel,
        grid=(geom.m // tm,),
        in_specs=[
            pl.BlockSpec((tm, D_MODEL), lambda i: (i, 0)),
            _const_spec((1, D_MODEL)),
            _mod_spec(layer, 0, geom, tm),
            _mod_spec(layer, 1, geom, tm),
        ],
        out_specs=pl.BlockSpec((tm, D_MODEL), lambda i: (i, 0)),
        out_shape=jax.ShapeDtypeStruct((geom.m, D_MODEL), BF16),
        compiler_params=_cparams("arbitrary"),
        name="modulate",
    )(x, norm_w, mods, mods)


def _lane_iota(rows):
    return lax.broadcasted_iota(jnp.int32, (rows, LANES), 1)


def _head_ones():
    i = np.arange(2 * HEAD_PAD)
    same_head = (i[:, None] // HEAD_PAD) == (i[None, :] // HEAD_PAD)
    return jnp.asarray(same_head & ((i % HEAD_PAD) < QK_HEAD)[:, None], BF16)


def _head_sumsq(x, ones_ref):
    w = ones_ref.shape[0]
    chunks = []
    for c in range(x.shape[1] // w):
        sq = x[:, c * w:(c + 1) * w]
        sq = sq * sq
        hi = sq.astype(BF16)
        lo = (sq - hi.astype(F32)).astype(BF16)
        chunks.append(jnp.dot(hi, ones_ref[...], preferred_element_type=F32)
                      + jnp.dot(lo, ones_ref[...], preferred_element_type=F32))
    return chunks


def _head_cols(chunks, hd):
    per = chunks[0].shape[1] // HEAD_PAD
    return chunks[hd // per][:, (hd % per) * HEAD_PAD:(hd % per + 1) * HEAD_PAD]


def _keys_values(ckv_bf, kpe_blk, tab, w_uk_ref, w_uv_ref, kvn_ref, kvp_ref, ones_ref, k_ref, v_ref):
    rows = ckv_bf.shape[0]
    lane = _lane_iota(rows)
    rope_lanes = (lane >= QK_NOPE) & (lane < QK_HEAD)
    vt = lax.dot_general(w_uv_ref[...], ckv_bf, _NT, preferred_element_type=F32)
    vrow = lax.broadcasted_iota(jnp.int32, vt.shape, 0)
    v_ref[...] = jnp.where(vrow % VT_HEAD == V_HEAD, 1.0, vt).astype(BF16)
    ssq_pe = jnp.sum(jnp.where(rope_lanes, kpe_blk * kpe_blk, 0.0), axis=-1, keepdims=True)
    t = kpe_blk * (kvp_ref[...] * tab)
    pe = jnp.where(rope_lanes, t + pltpu.roll(t, LANES - QK_ROPE, 1), 0.0)
    kn = jnp.dot(ckv_bf, w_uk_ref[...], preferred_element_type=F32)
    kvn = kvn_ref[...]
    ssq_k = _head_sumsq(kn, ones_ref)
    for hd in range(N_HEADS):
        sl = slice(hd * HEAD_PAD, (hd + 1) * HEAD_PAD)
        s = lax.rsqrt((_head_cols(ssq_k, hd) + ssq_pe) * (1.0 / QK_HEAD) + EPS)
        k_ref[:, sl] = (s * (kn[:, sl] * kvn + pe)).astype(BF16)


def _mla_kernel(h_ref, tab_ref, w_mla_ref, qa_ref, kva_ref, w_uq_ref, qv_ref, w_uk_ref, w_uv_ref,
                kvn_ref, kvp_ref, ones_ref, q_ref, k_ref, v_ref, ckv_ref, kpe_ref):
    z = jnp.dot(h_ref[...], w_mla_ref[...], preferred_element_type=F32)
    c_q = _rms(z[:, :OFF_CKV], qa_ref[...]).astype(BF16)
    c_kv = _rms(z[:, OFF_CKV:OFF_KPE], kva_ref[...])
    kpe_blk = z[:, OFF_KPE:]
    ckv_ref[...] = c_kv
    kpe_ref[...] = kpe_blk
    tab = tab_ref[...]
    rows = z.shape[0]
    lane = _lane_iota(rows)
    qraw = jnp.dot(c_q, w_uq_ref[...], preferred_element_type=F32)
    qt = qv_ref[...] * tab
    ssq_q = _head_sumsq(qraw, ones_ref)
    for hd in range(N_HEADS):
        sl = slice(hd * HEAD_PAD, (hd + 1) * HEAD_PAD)
        t = qraw[:, sl] * (lax.rsqrt(_head_cols(ssq_q, hd) * (1.0 / QK_HEAD) + EPS) * qt)
        r = pltpu.roll(t, LANES - QK_ROPE, 1)
        q_ref[:, sl] = jnp.where(lane < QK_NOPE, t, jnp.where(lane < QK_HEAD, t + r, 0.0)).astype(BF16)
    _keys_values(c_kv.astype(BF16), kpe_blk, tab, w_uk_ref, w_uv_ref, kvn_ref, kvp_ref, ones_ref, k_ref, v_ref)


def _mla_proj(h, tab_all, lw, geom):
    tm = geom.tile_tokenwise(256)
    n_tab = geom.dec_seq // tm

    def tab_map(i):
        t0 = i * tm
        return (jnp.where(t0 < geom.m_ctx, 0, 1 + ((t0 - geom.m_ctx) // tm) % n_tab), 0)

    row = lambda w: pl.BlockSpec((tm, w), lambda i: (i, 0))
    return pl.pallas_call(
        _mla_kernel,
        grid=(geom.m // tm,),
        in_specs=[
            row(D_MODEL),
            pl.BlockSpec((tm, LANES), tab_map),
            _const_spec((D_MODEL, MLA_W)),
            _const_spec((1, Q_LORA)),
            _const_spec((1, KV_LORA)),
            _const_spec((Q_LORA, QK_PAD)),
            _const_spec((1, LANES)),
            _const_spec((KV_LORA, QK_PAD)),
            _const_spec((VT_ROWS, KV_LORA)),
            _const_spec((1, LANES)),
            _const_spec((1, LANES)),
            _const_spec((2 * HEAD_PAD, 2 * HEAD_PAD)),
        ],
        out_specs=[row(QK_PAD), row(QK_PAD), pl.BlockSpec((VT_ROWS, tm), lambda i: (0, i)),
                   row(KV_LORA), row(LANES)],
        out_shape=[
            jax.ShapeDtypeStruct((geom.m, QK_PAD), BF16),
            jax.ShapeDtypeStruct((geom.m, QK_PAD), BF16),
            jax.ShapeDtypeStruct((VT_ROWS, geom.m), BF16),
            jax.ShapeDtypeStruct((geom.m, KV_LORA), F32),
            jax.ShapeDtypeStruct((geom.m, LANES), F32),
        ],
        compiler_params=_cparams("arbitrary"),
        name="mla_proj",
    )(h, tab_all, lw["w_mla"], lw["qa_w"], lw["kva_w"], lw["w_uq"], lw["qvec"], lw["w_uk"], lw["w_uv"],
      lw["kvec_nope"], lw["kvec_pe"], _head_ones())


def _cache_kv_kernel(ckv_ref, kpe_ref, tab_ref, w_uk_ref, w_uv_ref, kvn_ref, kvp_ref, ones_ref, k_ref, v_ref):
    _keys_values(ckv_ref[...].astype(BF16), kpe_ref[...], tab_ref[...], w_uk_ref, w_uv_ref,
                 kvn_ref, kvp_ref, ones_ref, k_ref, v_ref)


def _cache_kv(ckv, kpe_blk, tab_all, lw):
    rows = ckv.shape[0]
    tm = min(256, rows)
    assert rows % tm == 0 and tab_all.shape[0] >= tm
    row = lambda w: pl.BlockSpec((tm, w), lambda i: (i, 0))
    return pl.pallas_call(
        _cache_kv_kernel,
        grid=(rows // tm,),
        in_specs=[
            row(KV_LORA), row(LANES),
            pl.BlockSpec((tm, LANES), lambda i: (0, 0)),
            _const_spec((KV_LORA, QK_PAD)),
            _const_spec((VT_ROWS, KV_LORA)),
            _const_spec((1, LANES)),
            _const_spec((1, LANES)),
            _const_spec((2 * HEAD_PAD, 2 * HEAD_PAD)),
        ],
        out_specs=[row(QK_PAD), pl.BlockSpec((VT_ROWS, tm), lambda i: (0, i))],
        out_shape=[
            jax.ShapeDtypeStruct((rows, QK_PAD), BF16),
            jax.ShapeDtypeStruct((VT_ROWS, rows), BF16),
        ],
        compiler_params=_cparams("arbitrary"),
        name="cache_kv",
    )(ckv, kpe_blk, tab_all, lw["w_uk"], lw["w_uv"], lw["kvec_nope"], lw["kvec_pe"], _head_ones())


HEADS_PER_STEP = 2
KEY_CHUNK = 512
M_INIT = -1e30


def _attn_kernel(q_ref, k_ref, vt_ref, *rest):
    cache_refs, o_ref = rest[:-1], rest[-1]
    tq, seq_len = q_ref.shape[0], k_ref.shape[0]
    ck = min(KEY_CHUNK, seq_len)
    chunks = [(k_ref, vt_ref, c * ck, ck) for c in range(seq_len // ck)]
    if cache_refs:
        chunks.append((cache_refs[0], cache_refs[1], 0, cache_refs[0].shape[0]))
    heads = range(HEADS_PER_STEP)
    q = [q_ref[:, hh * HEAD_PAD:(hh + 1) * HEAD_PAD] for hh in heads]

    def scores(ci, hh):
        kr, _, r0, n = chunks[ci]
        return lax.dot_general(kr[r0:r0 + n, hh * HEAD_PAD:(hh + 1) * HEAD_PAD], q[hh], _NT,
                               preferred_element_type=F32)

    m = [jnp.full((1, tq), M_INIT, F32) for _ in heads]
    acc = [jnp.zeros((VT_HEAD, tq), F32) for _ in heads]
    s_next = [scores(0, hh) for hh in heads]
    for ci in range(len(chunks)):
        _, vr, r0, n = chunks[ci]
        s_cur = s_next
        if ci + 1 < len(chunks):
            s_next = [scores(ci + 1, hh) for hh in heads]
        for hh in heads:
            m_new = jnp.maximum(m[hh], jnp.max(s_cur[hh], axis=0, keepdims=True))
            alpha = jnp.exp2(m[hh] - m_new)
            p = jnp.exp2(s_cur[hh] - m_new).astype(BF16)
            vt = vr[hh * VT_HEAD:(hh + 1) * VT_HEAD, r0:r0 + n]
            acc[hh] = alpha * acc[hh] + jnp.dot(vt, p, preferred_element_type=F32)
            m[hh] = m_new
    outs = [a[:V_HEAD] / a[V_HEAD:V_HEAD + 1] for a in acc]
    o_ref[...] = jnp.concatenate(outs, axis=0).T.astype(BF16)


def _attention(q, k, vt, row0, n_seq, seq_len, tq, cache=None):
    assert row0 % seq_len == 0 and seq_len % tq == 0
    nq = seq_len // tq
    hp = N_HEADS // HEADS_PER_STEP
    qb0, sb0 = row0 // tq, row0 // seq_len
    kw, vw, vtw = HEADS_PER_STEP * HEAD_PAD, HEADS_PER_STEP * V_HEAD, HEADS_PER_STEP * VT_HEAD
    in_specs = [
        pl.BlockSpec((tq, kw), lambda b, h, i: (qb0 + b * nq + i, h)),
        pl.BlockSpec((seq_len, kw), lambda b, h, i: (sb0 + b, h)),
        pl.BlockSpec((vtw, seq_len), lambda b, h, i: (h, sb0 + b)),
    ]
    args = [q, k, vt]
    if cache is not None:
        k_c, vt_c, past = cache
        in_specs += [
            pl.BlockSpec((past, kw), lambda b, h, i: (b, h)),
            pl.BlockSpec((vtw, past), lambda b, h, i: (h, b)),
        ]
        args += [k_c, vt_c]
    return pl.pallas_call(
        _attn_kernel,
        grid=(n_seq, hp, nq),
        in_specs=in_specs,
        out_specs=pl.BlockSpec((tq, vw), lambda b, h, i: (b * nq + i, h)),
        out_shape=jax.ShapeDtypeStruct((n_seq * seq_len, ATTN_W), BF16),
        compiler_params=_cparams("arbitrary", "arbitrary", "arbitrary"),
        name="attention_cache" if cache is not None else "attention",
    )(*args)


def _shift_rows(x, d):
    n = x.shape[0]
    return pltpu.roll(x, (-d) % n, 0)


def _local_kernel(hp_ref, h_ref, hn_ref, w_ref, wpool_ref, pscale_ref, sgw_ref, wsp_ref, bsp_ref, cw_ref,
                  o_ref, *, tm, geom):
    i = pl.program_id(0)
    t0 = i * tm
    in_ctx = t0 < geom.m_ctx
    seq = jnp.where(in_ctx, geom.seq, geom.dec_seq)
    pos0 = jnp.where(in_ctx, t0 % geom.seq, (t0 - geom.m_ctx) % geom.dec_seq)
    rows = tm + 2 * HALO
    hext = jnp.concatenate([hp_ref[...], h_ref[...], hn_ref[...]], axis=0)
    z = jnp.dot(hext, w_ref[...], preferred_element_type=F32)
    p_ext = pos0 - HALO + lax.broadcasted_iota(jnp.int32, (rows, 1), 0)
    valid = (p_ext >= 0) & (p_ext < seq)
    inner = slice(HALO, HALO + tm)
    p = p_ext[inner]

    for g, w in enumerate(POOL_WINDOWS):
        x = jnp.where(valid, z[:, g * POOL_GROUP_W:(g + 1) * POOL_GROUP_W], 0.0)
        a, span = x, 1
        while span < w:
            a = a + _shift_rows(a, span)
            span *= 2
        win = _shift_rows(a, -(w // 2))[inner]
        lo = jnp.clip(p - w // 2, 0, seq - 1)
        hi = jnp.clip(p + w // 2 - 1, 0, seq - 1)
        cnt = (hi - lo + 1).astype(F32)
        pooled = win / cnt - x[inner]
        po = jnp.dot(pooled.astype(BF16), wpool_ref[g], preferred_element_type=F32)
        gs = slice(g * POOL_GROUP_W, (g + 1) * POOL_GROUP_W)
        o_ref[:, gs] = (po * pscale_ref[:, gs]).astype(BF16)

    zs = jax.nn.gelu(z[inner, POOL_W:POOL_W + 2 * SGU_W])
    u = zs[:, :SGU_W]
    vn = _rms(zs[:, SGU_W:], sgw_ref[...]).astype(BF16)
    gw = SGU_W // SGU_GROUPS
    for n in range(tm // CHUNK):
        rs = slice(n * CHUNK, (n + 1) * CHUNK)
        for g in range(SGU_GROUPS):
            cs = slice(g * gw, (g + 1) * gw)
            mixed = jnp.dot(wsp_ref[g], vn[rs, cs], preferred_element_type=F32) + bsp_ref[g]
            o_ref[rs, POOL_W + g * gw:POOL_W + (g + 1) * gw] = (u[rs, cs] * mixed).astype(BF16)

    c0 = POOL_W + 2 * SGU_W
    zc = z[:, c0:]
    y = jnp.where(valid, zc[:, CONV_W:2 * CONV_W] * zc[:, 2 * CONV_W:], 0.0)
    conv = (cw_ref[0:1, :] * _shift_rows(y, -1)[inner] + cw_ref[1:2, :] * y[inner]
            + cw_ref[2:3, :] * _shift_rows(y, 1)[inner])
    o_ref[:, POOL_W + SGU_W:] = (zc[inner, :CONV_W] * conv).astype(BF16)


def _local_branches(h, lw, geom):
    tm = geom.tile(256)
    assert tm % CHUNK == 0 and tm % HALO == 0
    nb = tm // HALO
    last = geom.m // HALO - 1
    return pl.pallas_call(
        functools.partial(_local_kernel, tm=tm, geom=geom),
        grid=(geom.m // tm,),
        in_specs=[
            pl.BlockSpec((HALO, D_MODEL), lambda i: (jnp.maximum(i * nb - 1, 0), 0)),
            pl.BlockSpec((tm, D_MODEL), lambda i: (i, 0)),
            pl.BlockSpec((HALO, D_MODEL), lambda i: (jnp.minimum((i + 1) * nb, last), 0)),
            _const_spec((D_MODEL, LOC_W)),
            _const_spec((len(POOL_WINDOWS), POOL_GROUP_W, POOL_GROUP_W)),
            _const_spec((1, POOL_W)),
            _const_spec((1, SGU_W)),
            _const_spec((SGU_GROUPS, CHUNK, CHUNK)),
            _const_spec((SGU_GROUPS, CHUNK, SGU_W // SGU_GROUPS)),
            _const_spec((3, CONV_W)),
        ],
        out_specs=pl.BlockSpec((tm, POOL_W + SGU_W + CONV_W), lambda i: (i, 0)),
        out_shape=jax.ShapeDtypeStruct((geom.m, POOL_W + SGU_W + CONV_W), BF16),
        compiler_params=_cparams("arbitrary"),
        name="local_branches",
    )(h, h, h, lw["w_loc"], lw["w_pool"], lw["pool_scale"], lw["sgu_norm_w"], lw["w_spatial"],
      lw["b_spatial"], lw["conv_w"])


def _merge_kernel(h_ref, actx_ref, alat_ref, br_ref, wg0, wg1, wg2, wg3, bg0, bg1, bg2, bg3,
                  wa_ref, wp_ref, ws_ref, wc_ref, o_ref, *, n_ctx_tiles):
    h = h_ref[...]
    attn = jnp.where(pl.program_id(0) < n_ctx_tiles, actx_ref[...], alat_ref[...])
    branches = (
        (attn, wa_ref, wg0, bg0),
        (br_ref[:, :POOL_W], wp_ref, wg1, bg1),
        (br_ref[:, POOL_W:POOL_W + SGU_W], ws_ref, wg2, bg2),
        (br_ref[:, POOL_W + SGU_W:], wc_ref, wg3, bg3),
    )
    acc = None
    for x, w_ref, wg_ref, bg_ref in branches:
        g = jax.nn.sigmoid(jnp.dot(h, wg_ref[...], preferred_element_type=F32) + bg_ref[...])
        t = g * jnp.dot(x, w_ref[...], preferred_element_type=F32)
        acc = t if acc is None else acc + t
    o_ref[...] = acc.astype(BF16)


def _merge(h, attn_ctx, attn_lat, br, lw, geom):
    tm = geom.tile_tokenwise(1024)
    tn = 256
    nn = D_MODEL // tn
    nctx = geom.m_ctx // tm
    gate_specs = [pl.BlockSpec((D_MODEL, tn), functools.partial(lambda b, i, j: (0, b * nn + j), b))
                  for b in range(N_BRANCH)]
    bias_specs = [pl.BlockSpec((1, tn), functools.partial(lambda b, i, j: (0, b * nn + j), b))
                  for b in range(N_BRANCH)]
    col = lambda k: pl.BlockSpec((k, tn), lambda i, j: (0, j))
    return pl.pallas_call(
        functools.partial(_merge_kernel, n_ctx_tiles=nctx),
        grid=(geom.m // tm, nn),
        in_specs=[
            pl.BlockSpec((tm, D_MODEL), lambda i, j: (i, 0)),
            pl.BlockSpec((tm, ATTN_W), lambda i, j: (jnp.minimum(i, nctx - 1), 0)),
            pl.BlockSpec((tm, ATTN_W), lambda i, j: (jnp.maximum(i - nctx, 0), 0)),
            pl.BlockSpec((tm, POOL_W + SGU_W + CONV_W), lambda i, j: (i, 0)),
            *gate_specs, *bias_specs,
            col(ATTN_W), col(POOL_W), col(SGU_W), col(CONV_W),
        ],
        out_specs=pl.BlockSpec((tm, tn), lambda i, j: (i, j)),
        out_shape=jax.ShapeDtypeStruct((geom.m, D_MODEL), BF16),
        compiler_params=_cparams("arbitrary", "arbitrary"),
        name="merge",
    )(h, attn_ctx, attn_lat, br, *([lw["w_gate"]] * N_BRANCH), *([lw["b_gate"]] * N_BRANCH),
      lw["w_br_attn"], lw["w_br_pool"], lw["w_br_sgu"], lw["w_br_conv"])


def _outproj_kernel(m_ref, w_ref, x_ref, ga_ref, nw_ref, sh_ref, sc_ref, x1_ref, h2_ref):
    mix = jnp.dot(m_ref[...], w_ref[...], preferred_element_type=F32)
    x1 = x_ref[...] + ga_ref[...] * mix
    x1_ref[...] = x1
    h2_ref[...] = _modulated(x1, nw_ref[...], sh_ref[...], sc_ref[...]).astype(BF16)


def _outproj(m, x, mods, lw, layer, geom):
    tm = geom.tile_tokenwise(256)
    row = lambda: pl.BlockSpec((tm, D_MODEL), lambda i: (i, 0))
    return pl.pallas_call(
        _outproj_kernel,
        grid=(geom.m // tm,),
        in_specs=[
            row(), _const_spec((D_MODEL, D_MODEL)), row(),
            _mod_spec(layer, 2, geom, tm),
            _const_spec((1, D_MODEL)),
            _mod_spec(layer, 3, geom, tm),
            _mod_spec(layer, 4, geom, tm),
        ],
        out_specs=[row(), row()],
        out_shape=[
            jax.ShapeDtypeStruct((geom.m, D_MODEL), F32),
            jax.ShapeDtypeStruct((geom.m, D_MODEL), BF16),
        ],
        compiler_params=_cparams("arbitrary"),
        name="outproj",
    )(m, lw["w_out"], x, mods, lw["norm_ffn_w"], mods, mods)


def _ffn_in_kernel(h_ref, wg_ref, wu_ref, o_ref):
    h = h_ref[...]
    a = jnp.dot(h, wg_ref[...], preferred_element_type=F32)
    b = jnp.dot(h, wu_ref[...], preferred_element_type=F32)
    o_ref[...] = (a * jax.nn.sigmoid(a) * b).astype(BF16)


def _ffn_in(h2, lw, geom):
    tm = geom.tile_tokenwise(1024)
    tf = 512
    nf = D_FF // tf
    return pl.pallas_call(
        _ffn_in_kernel,
        grid=(geom.m // tm, nf),
        in_specs=[
            pl.BlockSpec((tm, D_MODEL), lambda i, j: (i, 0)),
            pl.BlockSpec((D_MODEL, tf), lambda i, j: (0, j)),
            pl.BlockSpec((D_MODEL, tf), lambda i, j: (0, nf + j)),
        ],
        out_specs=pl.BlockSpec((tm, tf), lambda i, j: (i, j)),
        out_shape=jax.ShapeDtypeStruct((geom.m, D_FF), BF16),
        compiler_params=_cparams("arbitrary", "arbitrary"),
        name="ffn_in",
    )(h2, lw["w_ffn_in"], lw["w_ffn_in"])


def _ffn_out_kernel(a_ref, w_ref, x_ref, g_ref, o_ref):
    y = jnp.dot(a_ref[...], w_ref[...], preferred_element_type=F32)
    o_ref[...] = x_ref[...] + g_ref[...] * y


def _ffn_out(act, x1, mods, lw, layer, geom):
    tm = geom.tile_tokenwise(1024)
    tn = 256

    def g_map(i, j):
        return ((layer * MOD_ROWS + geom.mod_row(i * tm)) * N_MOD + 5, 0, j)

    return pl.pallas_call(
        _ffn_out_kernel,
        grid=(geom.m // tm, D_MODEL // tn),
        in_specs=[
            pl.BlockSpec((tm, D_FF), lambda i, j: (i, 0)),
            pl.BlockSpec((D_FF, tn), lambda i, j: (0, j)),
            pl.BlockSpec((tm, tn), lambda i, j: (i, j)),
            pl.BlockSpec((None, 1, tn), g_map),
        ],
        out_specs=pl.BlockSpec((tm, tn), lambda i, j: (i, j)),
        out_shape=jax.ShapeDtypeStruct((geom.m, D_MODEL), F32),
        compiler_params=_cparams("arbitrary", "arbitrary"),
        name="ffn_out",
    )(act, lw["w_ffn_out"], x1, mods)


def _rope_table(dec_seq, ident_rows):
    t = np.arange(dec_seq)
    row = (t // GRID_W).astype(np.float32)
    col = (t % GRID_W).astype(np.float32)
    inv = jnp.asarray(ROPE_THETA, F32) ** (-jnp.arange(AXIS_FREQS, dtype=F32) / AXIS_FREQS)
    ang_r = jnp.asarray(row)[:, None] * inv
    ang_c = jnp.asarray(col)[:, None] * inv
    ang = jnp.concatenate([ang_r, ang_r, ang_c, ang_c], axis=1)
    lat = jnp.concatenate([jnp.ones((dec_seq, QK_NOPE), F32), jnp.cos(ang), jnp.sin(ang)], axis=1)
    ident = jnp.concatenate([jnp.ones((ident_rows, QK_HEAD), F32), jnp.zeros((ident_rows, QK_ROPE), F32)], axis=1)
    return jnp.concatenate([ident, lat], axis=0)


def _prep_layer(l, w_in, b_gate, q_a_norm_w, kv_a_norm_w, w_uq, w_ukv, q_norm_w, k_norm_w, w_pool, pool_scale,
                sgu_norm_w, w_spatial, b_spatial, conv_w, w_br_attn, w_br_pool, w_br_sgu, w_br_conv, w_out,
                w_ffn_in, w_ffn_out, norm_ffn_w):
    perm, sign = _rope_perm()
    wi = w_in[l]
    w_kpe = wi[:, OFF_KPE:OFF_POOL]
    w_mla = jnp.concatenate(
        [wi[:, :OFF_KPE], jnp.zeros((D_MODEL, QK_NOPE), F32), w_kpe, w_kpe[:, perm] * sign], axis=1)
    uq = w_uq[l].reshape(Q_LORA, N_HEADS, QK_HEAD)
    uq_r = uq[:, :, QK_NOPE:]
    uq = jnp.concatenate([uq, uq_r[:, :, perm] * sign], axis=-1).reshape(Q_LORA, QK_PAD)
    qn = q_norm_w[l]
    qvec = (jnp.concatenate([qn, qn[QK_NOPE:][perm]]) * (ATTN_SCALE * LOG2E)).reshape(1, LANES)
    ukv = w_ukv[l].reshape(KV_LORA, N_HEADS, QK_NOPE + V_HEAD)
    uk = jnp.concatenate([ukv[:, :, :QK_NOPE], jnp.zeros((KV_LORA, N_HEADS, HEAD_PAD - QK_NOPE), F32)], axis=-1)
    kn = k_norm_w[l]
    zeros_nope = jnp.zeros((QK_NOPE,), F32)
    bsp = jnp.broadcast_to(b_spatial[l][:, :, None], (SGU_GROUPS, CHUNK, SGU_W // SGU_GROUPS))
    return dict(
        w_mla=w_mla.astype(BF16),
        qa_w=q_a_norm_w[l].reshape(1, Q_LORA),
        kva_w=kv_a_norm_w[l].reshape(1, KV_LORA),
        w_uq=uq.astype(BF16),
        qvec=qvec,
        w_uk=uk.reshape(KV_LORA, QK_PAD).astype(BF16),
        w_uv=jnp.pad(ukv[:, :, QK_NOPE:].transpose(1, 2, 0), ((0, 0), (0, VT_HEAD - V_HEAD), (0, 0))
                     ).reshape(VT_ROWS, KV_LORA).astype(BF16),
        kvec_nope=jnp.concatenate([kn[:QK_NOPE], zeros_nope]).reshape(1, LANES),
        kvec_pe=jnp.concatenate([zeros_nope, kn[QK_NOPE:], kn[QK_NOPE:][perm]]).reshape(1, LANES),
        w_loc=wi[:, OFF_POOL:OFF_GATE].astype(BF16),
        w_gate=wi[:, OFF_GATE:].astype(BF16),
        b_gate=b_gate[l].reshape(1, N_BRANCH * D_MODEL),
        w_pool=w_pool[l].astype(BF16),
        pool_scale=pool_scale[l].reshape(1, POOL_W),
        sgu_norm_w=sgu_norm_w[l].reshape(1, SGU_W),
        w_spatial=w_spatial[l].astype(BF16),
        b_spatial=bsp,
        conv_w=conv_w[l],
        w_br_attn=w_br_attn[l].astype(BF16),
        w_br_pool=w_br_pool[l].astype(BF16),
        w_br_sgu=w_br_sgu[l].astype(BF16),
        w_br_conv=w_br_conv[l].astype(BF16),
        w_out=w_out[l].astype(BF16),
        w_ffn_in=w_ffn_in[l].astype(BF16),
        w_ffn_out=w_ffn_out[l].astype(BF16),
        norm_ffn_w=norm_ffn_w[l].reshape(1, D_MODEL),
    )


def kernel(x_prompt, x_sample, cache_ckv, cache_kpe, c, c_ctx, w_mod, b_mod, norm_mix_w, norm_ffn_w, w_in, b_gate, q_a_norm_w, kv_a_norm_w, w_uq, w_ukv, q_norm_w, k_norm_w, w_pool, pool_scale, sgu_norm_w, w_spatial, b_spatial, conv_w, w_br_attn, w_br_pool, w_br_sgu, w_br_conv, w_out, w_ffn_in, w_ffn_out):
    batch, seq, _ = x_prompt.shape
    dec_batch, dec_seq, _ = x_sample.shape
    depth = w_in.shape[0]
    past = cache_ckv.shape[2]
    geom = _Geom(batch, seq, dec_batch, dec_seq)
    assert 1 + dec_batch <= MOD_ROWS

    cond = jnp.concatenate([c_ctx[None, :], c, jnp.zeros((MOD_ROWS - 1 - dec_batch, D_MODEL), F32)], axis=0)
    mods = _ada_params(cond, w_mod, b_mod.reshape(depth, 1, N_MOD * D_MODEL))
    mods = mods.reshape(depth * MOD_ROWS * N_MOD, 1, D_MODEL)

    tab_all = _rope_table(dec_seq, geom.tile_tokenwise(256))
    x = jnp.concatenate([x_prompt.reshape(geom.m_ctx, D_MODEL), x_sample.reshape(geom.m_lat, D_MODEL)], axis=0)
    tq_lat = min(512, dec_seq)
    tq_ctx = min(256, seq)

    ckv_states, kpe_states = [], []
    for l in range(depth):
        lw = _prep_layer(l, w_in, b_gate, q_a_norm_w, kv_a_norm_w, w_uq, w_ukv, q_norm_w, k_norm_w, w_pool,
                         pool_scale, sgu_norm_w, w_spatial, b_spatial, conv_w, w_br_attn, w_br_pool, w_br_sgu,
                         w_br_conv, w_out, w_ffn_in, w_ffn_out, norm_ffn_w)
        h = _modulate(x, mods, norm_mix_w[l].reshape(1, D_MODEL), l, geom)
        q, k, v, ckv, kpe = _mla_proj(h, tab_all, lw, geom)
        cache_kpe_blk = jnp.pad(cache_kpe[:, l].reshape(dec_batch * past, QK_ROPE),
                                ((0, 0), (QK_NOPE, LANES - QK_HEAD)))
        k_c, v_c = _cache_kv(cache_ckv[:, l].reshape(dec_batch * past, KV_LORA), cache_kpe_blk, tab_all, lw)
        attn_ctx = _attention(q, k, v, 0, batch, seq, tq_ctx)
        attn_lat = _attention(q, k, v, geom.m_ctx, dec_batch, dec_seq, tq_lat, cache=(k_c, v_c, past))
        br = _local_branches(h, lw, geom)
        m = _merge(h, attn_ctx, attn_lat, br, lw, geom)
        x1, h2 = _outproj(m, x, mods, lw, l, geom)
        act = _ffn_in(h2, lw, geom)
        x = _ffn_out(act, x1, mods, lw, l, geom)
        ckv_states.append(ckv[:geom.m_ctx].reshape(batch, seq, KV_LORA))
        kpe_states.append(kpe[:geom.m_ctx, QK_NOPE:QK_HEAD].reshape(batch, seq, QK_ROPE))

    y_prompt = x[:geom.m_ctx].reshape(batch, seq, D_MODEL)
    y_sample = x[geom.m_ctx:].reshape(dec_batch, dec_seq, D_MODEL)
    return (y_prompt, y_sample, jnp.stack(ckv_states, axis=1), jnp.stack(kpe_states, axis=1))
```

```python
import functools

import numpy as np
import jax
import jax.numpy as jnp
from jax import lax
from jax.experimental import pallas as pl
from jax.experimental.pallas import tpu as pltpu

F32 = jnp.float32
BF16 = jnp.bfloat16

D_MODEL = 2048
EPS = 1e-6
GRID_W = 64
N_HEADS = 16
QK_NOPE = 64
QK_ROPE = 32
QK_HEAD = QK_NOPE + QK_ROPE
V_HEAD = 64
Q_LORA = 768
KV_LORA = 256
ROPE_THETA = 10000.0
AXIS_FREQS = QK_ROPE // 4
ATTN_SCALE = QK_HEAD ** -0.5
ATTN_W = N_HEADS * V_HEAD
POOL_WINDOWS = (2, 4, 8, 16)
POOL_GROUP_W = 128
POOL_W = POOL_GROUP_W * len(POOL_WINDOWS)
CHUNK = 128
SGU_GROUPS = 4
SGU_W = 512
CONV_W = 512
N_BRANCH = 4
D_FF = 5632
OFF_CKV = Q_LORA
OFF_KPE = OFF_CKV + KV_LORA
OFF_POOL = OFF_KPE + QK_ROPE
OFF_SGU = OFF_POOL + POOL_W
OFF_CONV = OFF_SGU + 2 * SGU_W
OFF_GATE = OFF_CONV + 3 * CONV_W

LANES = 128
HEAD_PAD = LANES
QK_PAD = N_HEADS * HEAD_PAD
MLA_W = Q_LORA + KV_LORA + LANES
LOC_W = POOL_W + 2 * SGU_W + 3 * CONV_W
BF16_ROWS = 16
VT_HEAD = V_HEAD + BF16_ROWS
VT_ROWS = N_HEADS * VT_HEAD
HALO = BF16_ROWS
MLA_TM = 512
LOG2E = 1.4426950408889634
_NT = (((1,), (1,)), ((), ()))
N_MOD = 6
MOD_ROWS = 16
VMEM_LIMIT = 52 * 1024 * 1024


def _cparams(*sem):
    return pltpu.CompilerParams(dimension_semantics=sem, vmem_limit_bytes=VMEM_LIMIT)


def _rope_perm():
    j = np.arange(QK_ROPE)
    first_half = (j % (2 * AXIS_FREQS)) < AXIS_FREQS
    perm = np.where(first_half, j + AXIS_FREQS, j - AXIS_FREQS)
    sign = np.where(first_half, -1.0, 1.0).astype(np.float32)
    return perm, sign


def _ada_kernel(c_ref, w_ref, b_ref, o_ref):
    c = c_ref[...]
    s = c * jax.nn.sigmoid(c)
    s_hi = s.astype(BF16)
    s_lo = (s - s_hi.astype(F32)).astype(BF16)
    w = w_ref[...]
    w_hi = w.astype(BF16)
    w_lo = (w - w_hi.astype(F32)).astype(BF16)
    acc = jnp.dot(s_hi, w_hi, preferred_element_type=F32)
    acc += jnp.dot(s_lo, w_hi, preferred_element_type=F32)
    acc += jnp.dot(s_hi, w_lo, preferred_element_type=F32)
    o_ref[...] = acc + b_ref[...]


def _ada_params(cond, w_mod, b_mod):
    L = w_mod.shape[0]
    tn = 1024
    return pl.pallas_call(
        _ada_kernel,
        grid=(L, N_MOD * D_MODEL // tn),
        in_specs=[
            pl.BlockSpec((MOD_ROWS, D_MODEL), lambda l, j: (0, 0)),
            pl.BlockSpec((None, D_MODEL, tn), lambda l, j: (l, 0, j)),
            pl.BlockSpec((None, 1, tn), lambda l, j: (l, 0, j)),
        ],
        out_specs=pl.BlockSpec((None, MOD_ROWS, tn), lambda l, j: (l, 0, j)),
        out_shape=jax.ShapeDtypeStruct((L, MOD_ROWS, N_MOD * D_MODEL), F32),
        compiler_params=_cparams("arbitrary", "arbitrary"),
        name="ada_params",
    )(cond, w_mod, b_mod)


def _rms(x, w):
    return x * lax.rsqrt(jnp.mean(x * x, axis=-1, keepdims=True) + EPS) * w


def _modulated(x, nw, shift, scale):
    return _rms(x, nw) * (1.0 + scale) + shift


class _Geom:
    def __init__(self, batch, seq, dec_batch, dec_seq):
        self.seq, self.dec_seq = seq, dec_seq
        self.m_ctx = batch * seq
        self.m_lat = dec_batch * dec_seq
        self.m = self.m_ctx + self.m_lat

    def mod_row(self, t0):
        return jnp.where(t0 < self.m_ctx, 0, 1 + (t0 - self.m_ctx) // self.dec_seq)

    def tile(self, want):
        t = min(want, self.seq, self.dec_seq)
        assert self.m_ctx % t == 0 and self.m_lat % t == 0 and self.seq % t == 0 and self.dec_seq % t == 0
        return t

    def tile_tokenwise(self, want):
        t = want
        while self.m_ctx % t or self.dec_seq % t:
            t //= 2
        return t


def _mod_spec(layer, which, geom, tm):
    def imap(i, *_):
        return ((layer * MOD_ROWS + geom.mod_row(i * tm)) * N_MOD + which, 0, 0)
    return pl.BlockSpec((None, 1, D_MODEL), imap)


def _const_spec(shape):
    nd = len(shape)
    return pl.BlockSpec(shape, lambda *_: (0,) * nd)


def _modulate_kernel(x_ref, nw_ref, sh_ref, sc_ref, h_ref):
    h_ref[...] = _modulated(x_ref[...], nw_ref[...], sh_ref[...], sc_ref[...]).astype(BF16)


def _modulate(x, mods, norm_w, layer, geom):
    tm = geom.tile_tokenwise(512)
    return pl.pallas_call(
        _modulate_kernel,
        grid=(geom.m // tm,),
        in_specs=[
            pl.BlockSpec((tm, D_MODEL), lambda i: (i, 0)),
            _const_spec((1, D_MODEL)),
            _mod_spec(layer, 0, geom, tm),
            _mod_spec(layer, 1, geom, tm),
        ],
        out_specs=pl.BlockSpec((tm, D_MODEL), lambda i: (i, 0)),
        out_shape=jax.ShapeDtypeStruct((geom.m, D_MODEL), BF16),
        compiler_params=_cparams("arbitrary"),
        name="modulate",
    )(x, norm_w, mods, mods)


def _lane_iota(rows):
    return lax.broadcasted_iota(jnp.int32, (rows, LANES), 1)


def _head_ones():
    i = np.arange(2 * HEAD_PAD)
    same_head = (i[:, None] // HEAD_PAD) == (i[None, :] // HEAD_PAD)
    return jnp.asarray(same_head & ((i % HEAD_PAD) < QK_HEAD)[:, None], BF16)


def _head_sumsq(x, ones_ref):
    w = ones_ref.shape[0]
    chunks = []
    for c in range(x.shape[1] // w):
        sq = x[:, c * w:(c + 1) * w]
        sq = sq * sq
        hi = sq.astype(BF16)
        lo = (sq - hi.astype(F32)).astype(BF16)
        chunks.append(jnp.dot(hi, ones_ref[...], preferred_element_type=F32)
                      + jnp.dot(lo, ones_ref[...], preferred_element_type=F32))
    return chunks


def _head_cols(chunks, hd):
    per = chunks[0].shape[1] // HEAD_PAD
    return chunks[hd // per][:, (hd % per) * HEAD_PAD:(hd % per + 1) * HEAD_PAD]


def _keys_values(ckv_bf, kpe_blk, tab, w_uk_ref, w_uv_ref, kvn_ref, kvp_ref, ones_ref, k_ref, v_ref):
    rows = ckv_bf.shape[0]
    lane = _lane_iota(rows)
    rope_lanes = (lane >= QK_NOPE) & (lane < QK_HEAD)
    vt = lax.dot_general(w_uv_ref[...], ckv_bf, _NT, preferred_element_type=F32)
    vrow = lax.broadcasted_iota(jnp.int32, vt.shape, 0)
    v_ref[...] = jnp.where(vrow % VT_HEAD == V_HEAD, 1.0, vt).astype(BF16)
    ssq_pe = jnp.sum(jnp.where(rope_lanes, kpe_blk * kpe_blk, 0.0), axis=-1, keepdims=True)
    t = kpe_blk * (kvp_ref[...] * tab)
    pe = jnp.where(rope_lanes, t + pltpu.roll(t, LANES - QK_ROPE, 1), 0.0)
    kn = jnp.dot(ckv_bf, w_uk_ref[...], preferred_element_type=F32)
    kvn = kvn_ref[...]
    ssq_k = _head_sumsq(kn, ones_ref)
    for hd in range(N_HEADS):
        sl = slice(hd * HEAD_PAD, (hd + 1) * HEAD_PAD)
        s = lax.rsqrt((_head_cols(ssq_k, hd) + ssq_pe) * (1.0 / QK_HEAD) + EPS)
        k_ref[:, sl] = (s * (kn[:, sl] * kvn + pe)).astype(BF16)


def _mla_kernel(h_ref, tab_ref, w_mla_ref, qa_ref, kva_ref, w_uq_ref, qv_ref, w_uk_ref, w_uv_ref,
                kvn_ref, kvp_ref, ones_ref, q_ref, k_ref, v_ref, ckv_ref, kpe_ref):
    z = jnp.dot(h_ref[...], w_mla_ref[...], preferred_element_type=F32)
    c_q = _rms(z[:, :OFF_CKV], qa_ref[...]).astype(BF16)
    c_kv = _rms(z[:, OFF_CKV:OFF_KPE], kva_ref[...])
    kpe_blk = z[:, OFF_KPE:]
    ckv_ref[...] = c_kv
    kpe_ref[...] = kpe_blk
    tab = tab_ref[...]
    rows = z.shape[0]
    lane = _lane_iota(rows)
    qraw = jnp.dot(c_q, w_uq_ref[...], preferred_element_type=F32)
    qt = qv_ref[...] * tab
    ssq_q = _head_sumsq(qraw, ones_ref)
    for hd in range(N_HEADS):
        sl = slice(hd * HEAD_PAD, (hd + 1) * HEAD_PAD)
        t = qraw[:, sl] * (lax.rsqrt(_head_cols(ssq_q, hd) * (1.0 / QK_HEAD) + EPS) * qt)
        r = pltpu.roll(t, LANES - QK_ROPE, 1)
        q_ref[:, sl] = jnp.where(lane < QK_NOPE, t, jnp.where(lane < QK_HEAD, t + r, 0.0)).astype(BF16)
    _keys_values(c_kv.astype(BF16), kpe_blk, tab, w_uk_ref, w_uv_ref, kvn_ref, kvp_ref, ones_ref, k_ref, v_ref)


def _mla_proj(h, tab_all, lw, geom):
    tm = geom.tile_tokenwise(MLA_TM)
    n_tab = geom.dec_seq // tm

    def tab_map(i):
        t0 = i * tm
        return (jnp.where(t0 < geom.m_ctx, 0, 1 + ((t0 - geom.m_ctx) // tm) % n_tab), 0)

    row = lambda w: pl.BlockSpec((tm, w), lambda i: (i, 0))
    return pl.pallas_call(
        _mla_kernel,
        grid=(geom.m // tm,),
        in_specs=[
            row(D_MODEL),
            pl.BlockSpec((tm, LANES), tab_map),
            _const_spec((D_MODEL, MLA_W)),
            _const_spec((1, Q_LORA)),
            _const_spec((1, KV_LORA)),
            _const_spec((Q_LORA, QK_PAD)),
            _const_spec((1, LANES)),
            _const_spec((KV_LORA, QK_PAD)),
            _const_spec((VT_ROWS, KV_LORA)),
            _const_spec((1, LANES)),
            _const_spec((1, LANES)),
            _const_spec((2 * HEAD_PAD, 2 * HEAD_PAD)),
        ],
        out_specs=[row(QK_PAD), row(QK_PAD), pl.BlockSpec((VT_ROWS, tm), lambda i: (0, i)),
                   row(KV_LORA), row(LANES)],
        out_shape=[
            jax.ShapeDtypeStruct((geom.m, QK_PAD), BF16),
            jax.ShapeDtypeStruct((geom.m, QK_PAD), BF16),
            jax.ShapeDtypeStruct((VT_ROWS, geom.m), BF16),
            jax.ShapeDtypeStruct((geom.m, KV_LORA), F32),
            jax.ShapeDtypeStruct((geom.m, LANES), F32),
        ],
        compiler_params=_cparams("arbitrary"),
        name="mla_proj",
    )(h, tab_all, lw["w_mla"], lw["qa_w"], lw["kva_w"], lw["w_uq"], lw["qvec"], lw["w_uk"], lw["w_uv"],
      lw["kvec_nope"], lw["kvec_pe"], _head_ones())


def _cache_kv_kernel(ckv_ref, kpe_ref, tab_ref, w_uk_ref, w_uv_ref, kvn_ref, kvp_ref, ones_ref, k_ref, v_ref):
    _keys_values(ckv_ref[...].astype(BF16), kpe_ref[...], tab_ref[...], w_uk_ref, w_uv_ref,
                 kvn_ref, kvp_ref, ones_ref, k_ref, v_ref)


def _cache_kv(ckv, kpe_blk, tab_all, lw):
    rows = ckv.shape[0]
    tm = min(256, rows)
    assert rows % tm == 0 and tab_all.shape[0] >= tm
    row = lambda w: pl.BlockSpec((tm, w), lambda i: (i, 0))
    return pl.pallas_call(
        _cache_kv_kernel,
        grid=(rows // tm,),
        in_specs=[
            row(KV_LORA), row(LANES),
            pl.BlockSpec((tm, LANES), lambda i: (0, 0)),
            _const_spec((KV_LORA, QK_PAD)),
            _const_spec((VT_ROWS, KV_LORA)),
            _const_spec((1, LANES)),
            _const_spec((1, LANES)),
            _const_spec((2 * HEAD_PAD, 2 * HEAD_PAD)),
        ],
        out_specs=[row(QK_PAD), pl.BlockSpec((VT_ROWS, tm), lambda i: (0, i))],
        out_shape=[
            jax.ShapeDtypeStruct((rows, QK_PAD), BF16),
            jax.ShapeDtypeStruct((VT_ROWS, rows), BF16),
        ],
        compiler_params=_cparams("arbitrary"),
        name="cache_kv",
    )(ckv, kpe_blk, tab_all, lw["w_uk"], lw["w_uv"], lw["kvec_nope"], lw["kvec_pe"], _head_ones())


HEADS_PER_STEP = 2
KEY_CHUNK = 512
M_INIT = -1e30


def _attn_kernel(q_ref, k_ref, vt_ref, *rest):
    cache_refs, o_ref = rest[:-1], rest[-1]
    tq, seq_len = q_ref.shape[0], k_ref.shape[0]
    ck = min(KEY_CHUNK, seq_len)
    chunks = [(k_ref, vt_ref, c * ck, ck) for c in range(seq_len // ck)]
    if cache_refs:
        chunks.append((cache_refs[0], cache_refs[1], 0, cache_refs[0].shape[0]))
    heads = range(HEADS_PER_STEP)
    q = [q_ref[:, hh * HEAD_PAD:(hh + 1) * HEAD_PAD] for hh in heads]

    def scores(ci, hh):
        kr, _, r0, n = chunks[ci]
        return lax.dot_general(kr[r0:r0 + n, hh * HEAD_PAD:(hh + 1) * HEAD_PAD], q[hh], _NT,
                               preferred_element_type=F32)

    m = [jnp.full((1, tq), M_INIT, F32) for _ in heads]
    acc = [jnp.zeros((VT_HEAD, tq), F32) for _ in heads]
    s_next = [scores(0, hh) for hh in heads]
    for ci in range(len(chunks)):
        _, vr, r0, n = chunks[ci]
        s_cur = s_next
        if ci + 1 < len(chunks):
            s_next = [scores(ci + 1, hh) for hh in heads]
        for hh in heads:
            m_new = jnp.maximum(m[hh], jnp.max(s_cur[hh], axis=0, keepdims=True))
            alpha = jnp.exp2(m[hh] - m_new)
            p = jnp.exp2(s_cur[hh] - m_new).astype(BF16)
            vt = vr[hh * VT_HEAD:(hh + 1) * VT_HEAD, r0:r0 + n]
            acc[hh] = alpha * acc[hh] + jnp.dot(vt, p, preferred_element_type=F32)
            m[hh] = m_new
    outs = [a[:V_HEAD] / a[V_HEAD:V_HEAD + 1] for a in acc]
    o_ref[...] = jnp.concatenate(outs, axis=0).T.astype(BF16)


def _attention(q, k, vt, row0, n_seq, seq_len, tq, cache=None):
    assert row0 % seq_len == 0 and seq_len % tq == 0
    nq = seq_len // tq
    hp = N_HEADS // HEADS_PER_STEP
    qb0, sb0 = row0 // tq, row0 // seq_len
    kw, vw, vtw = HEADS_PER_STEP * HEAD_PAD, HEADS_PER_STEP * V_HEAD, HEADS_PER_STEP * VT_HEAD
    in_specs = [
        pl.BlockSpec((tq, kw), lambda b, h, i: (qb0 + b * nq + i, h)),
        pl.BlockSpec((seq_len, kw), lambda b, h, i: (sb0 + b, h)),
        pl.BlockSpec((vtw, seq_len), lambda b, h, i: (h, sb0 + b)),
    ]
    args = [q, k, vt]
    if cache is not None:
        k_c, vt_c, past = cache
        in_specs += [
            pl.BlockSpec((past, kw), lambda b, h, i: (b, h)),
            pl.BlockSpec((vtw, past), lambda b, h, i: (h, b)),
        ]
        args += [k_c, vt_c]
    return pl.pallas_call(
        _attn_kernel,
        grid=(n_seq, hp, nq),
        in_specs=in_specs,
        out_specs=pl.BlockSpec((tq, vw), lambda b, h, i: (b * nq + i, h)),
        out_shape=jax.ShapeDtypeStruct((n_seq * seq_len, ATTN_W), BF16),
        compiler_params=_cparams("arbitrary", "arbitrary", "arbitrary"),
        name="attention_cache" if cache is not None else "attention",
    )(*args)


def _shift_rows(x, d):
    n = x.shape[0]
    return pltpu.roll(x, (-d) % n, 0)


def _local_kernel(hp_ref, h_ref, hn_ref, w_ref, wpool_ref, pscale_ref, sgw_ref, wsp_ref, bsp_ref, cw_ref,
                  o_ref, *, tm, geom):
    i = pl.program_id(0)
    t0 = i * tm
    in_ctx = t0 < geom.m_ctx
    seq = jnp.where(in_ctx, geom.seq, geom.dec_seq)
    pos0 = jnp.where(in_ctx, t0 % geom.seq, (t0 - geom.m_ctx) % geom.dec_seq)
    rows = tm + 2 * HALO
    hext = jnp.concatenate([hp_ref[...], h_ref[...], hn_ref[...]], axis=0)
    z = jnp.dot(hext, w_ref[...], preferred_element_type=F32)
    p_ext = pos0 - HALO + lax.broadcasted_iota(jnp.int32, (rows, 1), 0)
    valid = (p_ext >= 0) & (p_ext < seq)
    inner = slice(HALO, HALO + tm)
    p = p_ext[inner]

    for g, w in enumerate(POOL_WINDOWS):
        x = jnp.where(valid, z[:, g * POOL_GROUP_W:(g + 1) * POOL_GROUP_W], 0.0)
        a, span = x, 1
        while span < w:
            a = a + _shift_rows(a, span)
            span *= 2
        win = _shift_rows(a, -(w // 2))[inner]
        lo = jnp.clip(p - w // 2, 0, seq - 1)
        hi = jnp.clip(p + w // 2 - 1, 0, seq - 1)
        cnt = (hi - lo + 1).astype(F32)
        pooled = win / cnt - x[inner]
        po = jnp.dot(pooled.astype(BF16), wpool_ref[g], preferred_element_type=F32)
        gs = slice(g * POOL_GROUP_W, (g + 1) * POOL_GROUP_W)
        o_ref[:, gs] = (po * pscale_ref[:, gs]).astype(BF16)

    zs = jax.nn.gelu(z[inner, POOL_W:POOL_W + 2 * SGU_W])
    u = zs[:, :SGU_W]
    vn = _rms(zs[:, SGU_W:], sgw_ref[...]).astype(BF16)
    gw = SGU_W // SGU_GROUPS
    for n in range(tm // CHUNK):
        rs = slice(n * CHUNK, (n + 1) * CHUNK)
        for g in range(SGU_GROUPS):
            cs = slice(g * gw, (g + 1) * gw)
            mixed = jnp.dot(wsp_ref[g], vn[rs, cs], preferred_element_type=F32) + bsp_ref[g]
            o_ref[rs, POOL_W + g * gw:POOL_W + (g + 1) * gw] = (u[rs, cs] * mixed).astype(BF16)

    c0 = POOL_W + 2 * SGU_W
    zc = z[:, c0:]
    y = jnp.where(valid, zc[:, CONV_W:2 * CONV_W] * zc[:, 2 * CONV_W:], 0.0)
    conv = (cw_ref[0:1, :] * _shift_rows(y, -1)[inner] + cw_ref[1:2, :] * y[inner]
            + cw_ref[2:3, :] * _shift_rows(y, 1)[inner])
    o_ref[:, POOL_W + SGU_W:] = (zc[inner, :CONV_W] * conv).astype(BF16)


def _local_branches(h, lw, geom):
    tm = geom.tile(256)
    assert tm % CHUNK == 0 and tm % HALO == 0
    nb = tm // HALO
    last = geom.m // HALO - 1
    return pl.pallas_call(
        functools.partial(_local_kernel, tm=tm, geom=geom),
        grid=(geom.m // tm,),
        in_specs=[
            pl.BlockSpec((HALO, D_MODEL), lambda i: (jnp.maximum(i * nb - 1, 0), 0)),
            pl.BlockSpec((tm, D_MODEL), lambda i: (i, 0)),
            pl.BlockSpec((HALO, D_MODEL), lambda i: (jnp.minimum((i + 1) * nb, last), 0)),
            _const_spec((D_MODEL, LOC_W)),
            _const_spec((len(POOL_WINDOWS), POOL_GROUP_W, POOL_GROUP_W)),
            _const_spec((1, POOL_W)),
            _const_spec((1, SGU_W)),
            _const_spec((SGU_GROUPS, CHUNK, CHUNK)),
            _const_spec((SGU_GROUPS, CHUNK, SGU_W // SGU_GROUPS)),
            _const_spec((3, CONV_W)),
        ],
        out_specs=pl.BlockSpec((tm, POOL_W + SGU_W + CONV_W), lambda i: (i, 0)),
        out_shape=jax.ShapeDtypeStruct((geom.m, POOL_W + SGU_W + CONV_W), BF16),
        compiler_params=_cparams("arbitrary"),
        name="local_branches",
    )(h, h, h, lw["w_loc"], lw["w_pool"], lw["pool_scale"], lw["sgu_norm_w"], lw["w_spatial"],
      lw["b_spatial"], lw["conv_w"])


def _merge_kernel(h_ref, actx_ref, alat_ref, br_ref, wg0, wg1, wg2, wg3, bg0, bg1, bg2, bg3,
                  wa_ref, wp_ref, ws_ref, wc_ref, o_ref, *, n_ctx_tiles):
    h = h_ref[...]
    attn = jnp.where(pl.program_id(0) < n_ctx_tiles, actx_ref[...], alat_ref[...])
    branches = (
        (attn, wa_ref, wg0, bg0),
        (br_ref[:, :POOL_W], wp_ref, wg1, bg1),
        (br_ref[:, POOL_W:POOL_W + SGU_W], ws_ref, wg2, bg2),
        (br_ref[:, POOL_W + SGU_W:], wc_ref, wg3, bg3),
    )
    acc = None
    for x, w_ref, wg_ref, bg_ref in branches:
        g = jax.nn.sigmoid(jnp.dot(h, wg_ref[...], preferred_element_type=F32) + bg_ref[...])
        t = g * jnp.dot(x, w_ref[...], preferred_element_type=F32)
        acc = t if acc is None else acc + t
    o_ref[...] = acc.astype(BF16)


def _merge(h, attn_ctx, attn_lat, br, lw, geom):
    tm = geom.tile_tokenwise(1024)
    tn = 256
    nn = D_MODEL // tn
    nctx = geom.m_ctx // tm
    gate_specs = [pl.BlockSpec((D_MODEL, tn), functools.partial(lambda b, i, j: (0, b * nn + j), b))
                  for b in range(N_BRANCH)]
    bias_specs = [pl.BlockSpec((1, tn), functools.partial(lambda b, i, j: (0, b * nn + j), b))
                  for b in range(N_BRANCH)]
    col = lambda k: pl.BlockSpec((k, tn), lambda i, j: (0, j))
    return pl.pallas_call(
        functools.partial(_merge_kernel, n_ctx_tiles=nctx),
        grid=(geom.m // tm, nn),
        in_specs=[
            pl.BlockSpec((tm, D_MODEL), lambda i, j: (i, 0)),
            pl.BlockSpec((tm, ATTN_W), lambda i, j: (jnp.minimum(i, nctx - 1), 0)),
            pl.BlockSpec((tm, ATTN_W), lambda i, j: (jnp.maximum(i - nctx, 0), 0)),
            pl.BlockSpec((tm, POOL_W + SGU_W + CONV_W), lambda i, j: (i, 0)),
            *gate_specs, *bias_specs,
            col(ATTN_W), col(POOL_W), col(SGU_W), col(CONV_W),
        ],
        out_specs=pl.BlockSpec((tm, tn), lambda i, j: (i, j)),
        out_shape=jax.ShapeDtypeStruct((geom.m, D_MODEL), BF16),
        compiler_params=_cparams("arbitrary", "arbitrary"),
        name="merge",
    )(h, attn_ctx, attn_lat, br, *([lw["w_gate"]] * N_BRANCH), *([lw["b_gate"]] * N_BRANCH),
      lw["w_br_attn"], lw["w_br_pool"], lw["w_br_sgu"], lw["w_br_conv"])


def _outproj_kernel(m_ref, w_ref, x_ref, ga_ref, nw_ref, sh_ref, sc_ref, x1_ref, h2_ref):
    mix = jnp.dot(m_ref[...], w_ref[...], preferred_element_type=F32)
    x1 = x_ref[...] + ga_ref[...] * mix
    x1_ref[...] = x1
    h2_ref[...] = _modulated(x1, nw_ref[...], sh_ref[...], sc_ref[...]).astype(BF16)


def _outproj(m, x, mods, lw, layer, geom):
    tm = geom.tile_tokenwise(256)
    row = lambda: pl.BlockSpec((tm, D_MODEL), lambda i: (i, 0))
    return pl.pallas_call(
        _outproj_kernel,
        grid=(geom.m // tm,),
        in_specs=[
            row(), _const_spec((D_MODEL, D_MODEL)), row(),
            _mod_spec(layer, 2, geom, tm),
            _const_spec((1, D_MODEL)),
            _mod_spec(layer, 3, geom, tm),
            _mod_spec(layer, 4, geom, tm),
        ],
        out_specs=[row(), row()],
        out_shape=[
            jax.ShapeDtypeStruct((geom.m, D_MODEL), F32),
            jax.ShapeDtypeStruct((geom.m, D_MODEL), BF16),
        ],
        compiler_params=_cparams("arbitrary"),
        name="outproj",
    )(m, lw["w_out"], x, mods, lw["norm_ffn_w"], mods, mods)


def _ffn_in_kernel(h_ref, wg_ref, wu_ref, o_ref):
    h = h_ref[...]
    a = jnp.dot(h, wg_ref[...], preferred_element_type=F32)
    b = jnp.dot(h, wu_ref[...], preferred_element_type=F32)
    o_ref[...] = (a * jax.nn.sigmoid(a) * b).astype(BF16)


def _ffn_in(h2, lw, geom):
    tm = geom.tile_tokenwise(1024)
    tf = 512
    nf = D_FF // tf
    return pl.pallas_call(
        _ffn_in_kernel,
        grid=(geom.m // tm, nf),
        in_specs=[
            pl.BlockSpec((tm, D_MODEL), lambda i, j: (i, 0)),
            pl.BlockSpec((D_MODEL, tf), lambda i, j: (0, j)),
            pl.BlockSpec((D_MODEL, tf), lambda i, j: (0, nf + j)),
        ],
        out_specs=pl.BlockSpec((tm, tf), lambda i, j: (i, j)),
        out_shape=jax.ShapeDtypeStruct((geom.m, D_FF), BF16),
        compiler_params=_cparams("arbitrary", "arbitrary"),
        name="ffn_in",
    )(h2, lw["w_ffn_in"], lw["w_ffn_in"])


def _ffn_out_kernel(a_ref, w_ref, x_ref, g_ref, o_ref):
    y = jnp.dot(a_ref[...], w_ref[...], preferred_element_type=F32)
    o_ref[...] = x_ref[...] + g_ref[...] * y


def _ffn_out(act, x1, mods, lw, layer, geom):
    tm = geom.tile_tokenwise(1024)
    tn = 256

    def g_map(i, j):
        return ((layer * MOD_ROWS + geom.mod_row(i * tm)) * N_MOD + 5, 0, j)

    return pl.pallas_call(
        _ffn_out_kernel,
        grid=(geom.m // tm, D_MODEL // tn),
        in_specs=[
            pl.BlockSpec((tm, D_FF), lambda i, j: (i, 0)),
            pl.BlockSpec((D_FF, tn), lambda i, j: (0, j)),
            pl.BlockSpec((tm, tn), lambda i, j: (i, j)),
            pl.BlockSpec((None, 1, tn), g_map),
        ],
        out_specs=pl.BlockSpec((tm, tn), lambda i, j: (i, j)),
        out_shape=jax.ShapeDtypeStruct((geom.m, D_MODEL), F32),
        compiler_params=_cparams("arbitrary", "arbitrary"),
        name="ffn_out",
    )(act, lw["w_ffn_out"], x1, mods)


def _rope_table(dec_seq, ident_rows):
    t = np.arange(dec_seq)
    row = (t // GRID_W).astype(np.float32)
    col = (t % GRID_W).astype(np.float32)
    inv = jnp.asarray(ROPE_THETA, F32) ** (-jnp.arange(AXIS_FREQS, dtype=F32) / AXIS_FREQS)
    ang_r = jnp.asarray(row)[:, None] * inv
    ang_c = jnp.asarray(col)[:, None] * inv
    ang = jnp.concatenate([ang_r, ang_r, ang_c, ang_c], axis=1)
    lat = jnp.concatenate([jnp.ones((dec_seq, QK_NOPE), F32), jnp.cos(ang), jnp.sin(ang)], axis=1)
    ident = jnp.concatenate([jnp.ones((ident_rows, QK_HEAD), F32), jnp.zeros((ident_rows, QK_ROPE), F32)], axis=1)
    return jnp.concatenate([ident, lat], axis=0)


def _prep_layer(l, w_in, b_gate, q_a_norm_w, kv_a_norm_w, w_uq, w_ukv, q_norm_w, k_norm_w, w_pool, pool_scale,
                sgu_norm_w, w_spatial, b_spatial, conv_w, w_br_attn, w_br_pool, w_br_sgu, w_br_conv, w_out,
                w_ffn_in, w_ffn_out, norm_ffn_w):
    perm, sign = _rope_perm()
    wi = w_in[l]
    w_kpe = wi[:, OFF_KPE:OFF_POOL]
    w_mla = jnp.concatenate(
        [wi[:, :OFF_KPE], jnp.zeros((D_MODEL, QK_NOPE), F32), w_kpe, w_kpe[:, perm] * sign], axis=1)
    uq = w_uq[l].reshape(Q_LORA, N_HEADS, QK_HEAD)
    uq_r = uq[:, :, QK_NOPE:]
    uq = jnp.concatenate([uq, uq_r[:, :, perm] * sign], axis=-1).reshape(Q_LORA, QK_PAD)
    qn = q_norm_w[l]
    qvec = (jnp.concatenate([qn, qn[QK_NOPE:][perm]]) * (ATTN_SCALE * LOG2E)).reshape(1, LANES)
    ukv = w_ukv[l].reshape(KV_LORA, N_HEADS, QK_NOPE + V_HEAD)
    uk = jnp.concatenate([ukv[:, :, :QK_NOPE], jnp.zeros((KV_LORA, N_HEADS, HEAD_PAD - QK_NOPE), F32)], axis=-1)
    kn = k_norm_w[l]
    zeros_nope = jnp.zeros((QK_NOPE,), F32)
    bsp = jnp.broadcast_to(b_spatial[l][:, :, None], (SGU_GROUPS, CHUNK, SGU_W // SGU_GROUPS))
    return dict(
        w_mla=w_mla.astype(BF16),
        qa_w=q_a_norm_w[l].reshape(1, Q_LORA),
        kva_w=kv_a_norm_w[l].reshape(1, KV_LORA),
        w_uq=uq.astype(BF16),
        qvec=qvec,
        w_uk=uk.reshape(KV_LORA, QK_PAD).astype(BF16),
        w_uv=jnp.pad(ukv[:, :, QK_NOPE:].transpose(1, 2, 0), ((0, 0), (0, VT_HEAD - V_HEAD), (0, 0))
                     ).reshape(VT_ROWS, KV_LORA).astype(BF16),
        kvec_nope=jnp.concatenate([kn[:QK_NOPE], zeros_nope]).reshape(1, LANES),
        kvec_pe=jnp.concatenate([zeros_nope, kn[QK_NOPE:], kn[QK_NOPE:][perm]]).reshape(1, LANES),
        w_loc=wi[:, OFF_POOL:OFF_GATE].astype(BF16),
        w_gate=wi[:, OFF_GATE:].astype(BF16),
        b_gate=b_gate[l].reshape(1, N_BRANCH * D_MODEL),
        w_pool=w_pool[l].astype(BF16),
        pool_scale=pool_scale[l].reshape(1, POOL_W),
        sgu_norm_w=sgu_norm_w[l].reshape(1, SGU_W),
        w_spatial=w_spatial[l].astype(BF16),
        b_spatial=bsp,
        conv_w=conv_w[l],
        w_br_attn=w_br_attn[l].astype(BF16),
        w_br_pool=w_br_pool[l].astype(BF16),
        w_br_sgu=w_br_sgu[l].astype(BF16),
        w_br_conv=w_br_conv[l].astype(BF16),
        w_out=w_out[l].astype(BF16),
        w_ffn_in=w_ffn_in[l].astype(BF16),
        w_ffn_out=w_ffn_out[l].astype(BF16),
        norm_ffn_w=norm_ffn_w[l].reshape(1, D_MODEL),
    )


def kernel(x_prompt, x_sample, cache_ckv, cache_kpe, c, c_ctx, w_mod, b_mod, norm_mix_w, norm_ffn_w, w_in, b_gate, q_a_norm_w, kv_a_norm_w, w_uq, w_ukv, q_norm_w, k_norm_w, w_pool, pool_scale, sgu_norm_w, w_spatial, b_spatial, conv_w, w_br_attn, w_br_pool, w_br_sgu, w_br_conv, w_out, w_ffn_in, w_ffn_out):
    batch, seq, _ = x_prompt.shape
    dec_batch, dec_seq, _ = x_sample.shape
    depth = w_in.shape[0]
    past = cache_ckv.shape[2]
    geom = _Geom(batch, seq, dec_batch, dec_seq)
    assert 1 + dec_batch <= MOD_ROWS

    cond = jnp.concatenate([c_ctx[None, :], c, jnp.zeros((MOD_ROWS - 1 - dec_batch, D_MODEL), F32)], axis=0)
    mods = _ada_params(cond, w_mod, b_mod.reshape(depth, 1, N_MOD * D_MODEL))
    mods = mods.reshape(depth * MOD_ROWS * N_MOD, 1, D_MODEL)

    tab_all = _rope_table(dec_seq, geom.tile_tokenwise(MLA_TM))
    x = jnp.concatenate([x_prompt.reshape(geom.m_ctx, D_MODEL), x_sample.reshape(geom.m_lat, D_MODEL)], axis=0)
    tq_lat = min(512, dec_seq)
    tq_ctx = min(256, seq)

    ckv_states, kpe_states = [], []
    for l in range(depth):
        lw = _prep_layer(l, w_in, b_gate, q_a_norm_w, kv_a_norm_w, w_uq, w_ukv, q_norm_w, k_norm_w, w_pool,
                         pool_scale, sgu_norm_w, w_spatial, b_spatial, conv_w, w_br_attn, w_br_pool, w_br_sgu,
                         w_br_conv, w_out, w_ffn_in, w_ffn_out, norm_ffn_w)
        h = _modulate(x, mods, norm_mix_w[l].reshape(1, D_MODEL), l, geom)
        q, k, v, ckv, kpe = _mla_proj(h, tab_all, lw, geom)
        cache_kpe_blk = jnp.pad(cache_kpe[:, l].reshape(dec_batch * past, QK_ROPE),
                                ((0, 0), (QK_NOPE, LANES - QK_HEAD)))
        k_c, v_c = _cache_kv(cache_ckv[:, l].reshape(dec_batch * past, KV_LORA), cache_kpe_blk, tab_all, lw)
        attn_ctx = _attention(q, k, v, 0, batch, seq, tq_ctx)
        attn_lat = _attention(q, k, v, geom.m_ctx, dec_batch, dec_seq, tq_lat, cache=(k_c, v_c, past))
        br = _local_branches(h, lw, geom)
        m = _merge(h, attn_ctx, attn_lat, br, lw, geom)
        x1, h2 = _outproj(m, x, mods, lw, l, geom)
        act = _ffn_in(h2, lw, geom)
        x = _ffn_out(act, x1, mods, lw, l, geom)
        ckv_states.append(ckv[:geom.m_ctx].reshape(batch, seq, KV_LORA))
        kpe_states.append(kpe[:geom.m_ctx, QK_NOPE:QK_HEAD].reshape(batch, seq, QK_ROPE))

    y_prompt = x[:geom.m_ctx].reshape(batch, seq, D_MODEL)
    y_sample = x[geom.m_ctx:].reshape(dec_batch, dec_seq, D_MODEL)
    return (y_prompt, y_sample, jnp.stack(ckv_states, axis=1), jnp.stack(kpe_states, axis=1))
```
